```python
import math
import jax, jax.numpy as jnp
from jax import lax
import numpy as np

D_MODEL = 1024
BATCH = 8
SEQ = 2048
DEPTH = 2
DEC_BATCH = 128
DEC_SEQ = 8
PAST_LEN = 16384
PAGE_SIZE = 128

N_MIXERS = 2
N_GLA = (DEPTH + 1) // 2
N_CMLP = DEPTH // 2
GLA_HEADS = 4
GLA_DK = D_MODEL // 2 // GLA_HEADS
GLA_DV = D_MODEL // GLA_HEADS
GLA_KW = GLA_HEADS * GLA_DK
GLA_VW = GLA_HEADS * GLA_DV
GLA_GATE_RANK = 16
GLA_GATE_TAU = 16.0
GLA_CHUNK = 64
GLA_IN_WIDTH = 2 * GLA_KW + 2 * GLA_VW + GLA_GATE_RANK
CMLP_WIDTH = D_MODEL
CMLP_GROUPS = 4
CMLP_GROUP_DIM = CMLP_WIDTH // CMLP_GROUPS
CMLP_CHUNK = 128
EPS = 1e-6

kernel_name = "hybrid_gla_chunkmlp_decode_step"


def rmsnorm(x, g):
    xf = x.astype(jnp.float32)
    y = xf * lax.rsqrt(jnp.mean(xf * xf, axis=-1, keepdims=True) + EPS)
    return (y * g.astype(jnp.float32)).astype(x.dtype)


def layernorm(x, g, b):
    xf = x.astype(jnp.float32)
    mu = jnp.mean(xf, axis=-1, keepdims=True)
    var = jnp.mean(jnp.square(xf - mu), axis=-1, keepdims=True)
    y = (xf - mu) * lax.rsqrt(var + EPS)
    return (y * g.astype(jnp.float32) + b.astype(jnp.float32)).astype(x.dtype)


def gla_chunked(q, k, v, log_a, s0, chunk):
    b, t, h, dk = q.shape
    dv = v.shape[-1]
    n = t // chunk

    def to_chunks(z):
        return z.astype(jnp.float32).reshape(b, n, chunk, h, z.shape[-1]).transpose(1, 0, 3, 2, 4)

    qc, kc, vc, ac = to_chunks(q), to_chunks(k), to_chunks(v), to_chunks(log_a)
    causal = jnp.tril(jnp.ones((chunk, chunk), dtype=bool))

    def step(state, inp):
        qi, ki, vi, ai = inp
        cum = jnp.cumsum(ai, axis=2)
        diff = cum[:, :, :, None, :] - cum[:, :, None, :, :]
        decay = jnp.exp(jnp.where(causal[:, :, None], diff, -jnp.inf))
        scores = jnp.einsum('bhtd,bhsd,bhtsd->bhts', qi, ki, decay)
        o_intra = jnp.einsum('bhts,bhsv->bhtv', scores, vi)
        o_inter = jnp.einsum('bhtd,bhdv->bhtv', qi * jnp.exp(cum), state)
        last = cum[:, :, -1:, :]
        k_dec = ki * jnp.exp(last - cum)
        new_state = jnp.exp(last[:, :, 0, :])[..., None] * state + jnp.einsum('bhsd,bhsv->bhdv', k_dec, vi)
        return new_state, o_intra + o_inter

    s_final, o = lax.scan(step, s0.astype(jnp.float32), (qc, kc, vc, ac))
    o = o.transpose(1, 0, 3, 2, 4).reshape(b, t, h, dv)
    return o, s_final


def gla_mixer(h, s0, w_in, w_a_up, b_a_up, g_onorm, w_out):
    bsz, t, _ = h.shape
    proj = h @ w_in
    q, k, v, gate, a_low = jnp.split(
        proj, [GLA_KW, 2 * GLA_KW, 2 * GLA_KW + GLA_VW, 2 * GLA_KW + 2 * GLA_VW], axis=-1)
    q = q.reshape(bsz, t, GLA_HEADS, GLA_DK) * (GLA_DK ** -0.5)
    k = k.reshape(bsz, t, GLA_HEADS, GLA_DK)
    v = v.reshape(bsz, t, GLA_HEADS, GLA_DV)
    gate_logit = (a_low @ w_a_up + b_a_up).astype(jnp.float32)
    log_a = (jax.nn.log_sigmoid(gate_logit) / GLA_GATE_TAU).reshape(bsz, t, GLA_HEADS, GLA_DK)
    o, s_new = gla_chunked(q, k, v, log_a, s0, math.gcd(t, GLA_CHUNK))
    o = rmsnorm(o, g_onorm).astype(h.dtype)
    o = o.reshape(bsz, t, GLA_VW) * jax.nn.silu(gate)
    return o @ w_out, s_new


def cmlp_mixer(h, w_in, ln_g, ln_b, w_spatial, b_spatial, w_out):
    bsz, t, _ = h.shape
    u, v, gate = jnp.split(h @ w_in, 3, axis=-1)
    v = layernorm(v, ln_g, ln_b)
    c = min(t, CMLP_CHUNK)
    n = t // c
    mask = jnp.tril(jnp.ones((c, c), dtype=w_spatial.dtype))
    ws = w_spatial[:, :c, :c] * mask
    bs = b_spatial[:, :c]
    vg = v.reshape(bsz, n, c, CMLP_GROUPS, CMLP_GROUP_DIM)
    mixed = jnp.einsum('gts,bnsgd->bntgd', ws, vg) + bs.T[:, :, None]
    z = u * mixed.reshape(bsz, t, CMLP_WIDTH) * jax.nn.silu(gate)
    return z @ w_out, v


def trunk(x, gla_states, norm_g, gla_w_in, gla_w_a_up, gla_b_a_up, gla_g_onorm, gla_w_out,
          cmlp_w_in, cmlp_ln_g, cmlp_ln_b, cmlp_w_spatial, cmlp_b_spatial, cmlp_w_out, norm_final):
    new_gla, new_v = [], []
    for i in range(DEPTH):
        h = rmsnorm(x, norm_g[i])
        j = i // N_MIXERS
        if i % N_MIXERS == 0:
            y, s = gla_mixer(h, gla_states[j], gla_w_in[j], gla_w_a_up[j], gla_b_a_up[j],
                             gla_g_onorm[j], gla_w_out[j])
            new_gla.append(s)
        else:
            y, vrows = cmlp_mixer(h, cmlp_w_in[j], cmlp_ln_g[j], cmlp_ln_b[j], cmlp_w_spatial[j],
                                  cmlp_b_spatial[j], cmlp_w_out[j])
            new_v.append(vrows)
        x = x + y
    return rmsnorm(x, norm_final), jnp.stack(new_gla), jnp.stack(new_v)


def setup_inputs(seed: int = 0) -> dict:
    key = jax.random.key(seed)
    ks = jax.random.split(key, 20)
    f32 = jnp.float32
    nrm = lambda k, s, sc: jax.random.normal(k, s, f32) * sc
    return {
        "x_prompt": nrm(ks[0], (BATCH, SEQ, D_MODEL), 1.0),
        "x_sample": nrm(ks[1], (DEC_BATCH, DEC_SEQ, D_MODEL), 1.0),
        "state_gla": nrm(ks[2], (N_GLA, DEC_BATCH, GLA_HEADS, GLA_DK, GLA_DV), 0.5),
        "norm_g": 1.0 + nrm(ks[3], (DEPTH, D_MODEL), 0.02),
        "gla_w_in": nrm(ks[4], (N_GLA, D_MODEL, GLA_IN_WIDTH), D_MODEL ** -0.5),
        "gla_w_a_up": nrm(ks[5], (N_GLA, GLA_GATE_RANK, GLA_KW), GLA_GATE_RANK ** -0.5),
        "gla_b_a_up": nrm(ks[6], (N_GLA, GLA_KW), 0.1),
        "gla_g_onorm": 1.0 + nrm(ks[7], (N_GLA, GLA_DV), 0.02),
        "gla_w_out": nrm(ks[8], (N_GLA, GLA_VW, D_MODEL), GLA_VW ** -0.5),
        "cmlp_w_in": nrm(ks[9], (N_CMLP, D_MODEL, 3 * CMLP_WIDTH), D_MODEL ** -0.5),
        "cmlp_ln_g": 1.0 + nrm(ks[10], (N_CMLP, CMLP_WIDTH), 0.02),
        "cmlp_ln_b": nrm(ks[11], (N_CMLP, CMLP_WIDTH), 0.02),
        "cmlp_w_spatial": nrm(ks[12], (N_CMLP, CMLP_GROUPS, CMLP_CHUNK, CMLP_CHUNK), CMLP_CHUNK ** -0.5),
        "cmlp_b_spatial": 1.0 + nrm(ks[13], (N_CMLP, CMLP_GROUPS, CMLP_CHUNK), 0.02),
        "cmlp_w_out": nrm(ks[14], (N_CMLP, CMLP_WIDTH, D_MODEL), CMLP_WIDTH ** -0.5),
        "norm_final": 1.0 + nrm(ks[15], (D_MODEL,), 0.02),
    }


def reference(x_prompt, x_sample, state_gla, norm_g, gla_w_in, gla_w_a_up, gla_b_a_up, gla_g_onorm,
              gla_w_out, cmlp_w_in, cmlp_ln_g, cmlp_ln_b, cmlp_w_spatial, cmlp_b_spatial, cmlp_w_out,
              norm_final):
    weights = (norm_g, gla_w_in, gla_w_a_up, gla_b_a_up, gla_g_onorm, gla_w_out,
               cmlp_w_in, cmlp_ln_g, cmlp_ln_b, cmlp_w_spatial, cmlp_b_spatial, cmlp_w_out, norm_final)
    zero_state = jnp.zeros((N_GLA, x_prompt.shape[0], GLA_HEADS, GLA_DK, GLA_DV), jnp.float32)
    y_prompt, gla_state_prompt, _ = trunk(x_prompt, zero_state, *weights)
    y_sample, gla_state_sample, cmlp_v_sample = trunk(x_sample, state_gla, *weights)
    return (y_prompt, y_sample, gla_state_prompt, gla_state_sample, cmlp_v_sample)
```

```python
import functools

import jax
import jax.numpy as jnp
from jax import lax
from jax.experimental import pallas as pl
from jax.experimental.pallas import tpu as pltpu

F32 = jnp.float32
BF16 = jnp.bfloat16

D_MODEL = 1024
GLA_HEADS = 4
GLA_DK = 128
GLA_DV = 256
GLA_KW = GLA_HEADS * GLA_DK
GLA_VW = GLA_HEADS * GLA_DV
GLA_GATE_RANK = 16
GLA_GATE_TAU = 16.0
GLA_IN_PAD = 2 * GLA_KW + 2 * GLA_VW + 128
CMLP_WIDTH = 1024
CMLP_GROUPS = 4
CMLP_GROUP_DIM = CMLP_WIDTH // CMLP_GROUPS
CHUNK = 128
SUB = 8
EPS = 1e-6
VMEM_LIMIT = 56 * 1024 * 1024


def _dot(a, b):
    return jnp.dot(a, b, preferred_element_type=F32)


def _dot_nt(a, b):
    return lax.dot_general(a, b, (((1,), (1,)), ((), ())), preferred_element_type=F32)


def _rms(x, g):
    return x * lax.rsqrt(jnp.mean(x * x, axis=-1, keepdims=True) + EPS) * g


def _silu(x):
    return x * jax.nn.sigmoid(x)


def _split3(x):
    hi = x.astype(BF16)
    r1 = x - hi.astype(F32)
    mid = r1.astype(BF16)
    lo = (r1 - mid.astype(F32)).astype(BF16)
    return hi, mid, lo


def _masked_prefix_sum(mask01, x):
    hi, mid, lo = _split3(x)
    return _dot(mask01, hi) + _dot(mask01, mid) + _dot(mask01, lo)


def _sublane_bcast(x, row, group):
    r, n = x.shape
    x3 = x.reshape(r // group, group, n)
    return jnp.broadcast_to(x3[:, row:row + 1, :], x3.shape).reshape(r, n)


def _gla_in_kernel(x_ref, g_ref, w_ref, wa_ref, ba_ref, q_ref, k_ref, v_ref, gate_ref, la_ref):
    hb = _rms(x_ref[...], g_ref[...]).astype(BF16)
    q_ref[...] = _dot(hb, w_ref[:, 0:GLA_KW]) * (GLA_DK ** -0.5)
    k_ref[...] = _dot(hb, w_ref[:, GLA_KW:2 * GLA_KW])
    v_ref[...] = _dot(hb, w_ref[:, 2 * GLA_KW:2 * GLA_KW + GLA_VW])
    gate_ref[...] = _silu(_dot(hb, w_ref[:, 2 * GLA_KW + GLA_VW:2 * GLA_KW + 2 * GLA_VW]))
    a_low = _dot(hb, w_ref[:, 2 * GLA_KW + 2 * GLA_VW:GLA_IN_PAD])
    z = _dot(a_low.astype(BF16), wa_ref[...]) + ba_ref[...]
    log_sig = jnp.minimum(z, 0.0) - jnp.log1p(jnp.exp(-jnp.abs(z)))
    la_ref[...] = log_sig * (1.0 / GLA_GATE_TAU)


def _const_spec(shape):
    nd = len(shape)
    return pl.BlockSpec(shape, lambda *_: (0,) * nd, pipeline_mode=pl.Buffered(1))


def _gla_in(x2d, norm_g, w_in_p, wa_p, ba, tm):
    n = x2d.shape[0]
    row = lambda w: pl.BlockSpec((tm, w), lambda i: (i, 0))
    return pl.pallas_call(
        _gla_in_kernel,
        grid=(n // tm,),
        in_specs=[row(D_MODEL), _const_spec((1, D_MODEL)), _const_spec((D_MODEL, GLA_IN_PAD)),
                  _const_spec((128, GLA_KW)), _const_spec((1, GLA_KW))],
        out_specs=[row(GLA_KW), row(GLA_KW), row(GLA_VW), row(GLA_VW), row(GLA_KW)],
        out_shape=[jax.ShapeDtypeStruct((n, GLA_KW), F32), jax.ShapeDtypeStruct((n, GLA_KW), F32),
                   jax.ShapeDtypeStruct((n, GLA_VW), F32), jax.ShapeDtypeStruct((n, GLA_VW), F32),
                   jax.ShapeDtypeStruct((n, GLA_KW), F32)],
        compiler_params=pltpu.CompilerParams(dimension_semantics=("arbitrary",),
                                             vmem_limit_bytes=VMEM_LIMIT),
        name="gla_in",
    )(x2d, norm_g, w_in_p, wa_p, ba)


def _intra_scores(q, k, cum, levels, rxc, lower):
    r = q.shape[0]
    parts = []
    for s in range(SUB):
        kb = _sublane_bcast(k, s, SUB)
        cb = _sublane_bcast(cum, s, SUB)
        parts.append((q * kb * jnp.exp(jnp.minimum(cum - cb, 0.0))).astype(BF16))
    acat = jnp.concatenate(parts, axis=1)
    pr = lax.broadcasted_iota(jnp.int32, (SUB * GLA_DK, r), 0)
    pc = lax.broadcasted_iota(jnp.int32, (SUB * GLA_DK, r), 1)
    lane_sum = ((pc & (SUB - 1)) == (pr >> 7)).astype(BF16)
    scores = jnp.where((rxc < SUB) & lower, _dot(acat, lane_sum), 0.0)
    rows = lax.broadcasted_iota(jnp.int32, (r, 1), 0)
    for m in levels:
        ref = _sublane_bcast(cum, m - 1, 2 * m)
        e = jnp.exp(-jnp.abs(cum - ref))
        upper = (rows & (2 * m - 1)) >= m
        qd = jnp.where(upper, q * e, 0.0).astype(BF16)
        kd = jnp.where(upper, 0.0, k * e).astype(BF16)
        scores = jnp.where((rxc >= m) & (rxc < 2 * m) & lower, _dot_nt(qd, kd), scores)
    return scores


def _out_norm_gate(o, gon, gate):
    return (_rms(o, gon) * gate).astype(BF16)


def _gla_prompt_kernel(q_ref, k_ref, v_ref, gate_ref, la_ref, gon_ref, og_ref, s_ref):
    @pl.when(pl.program_id(1) == 0)
    def _():
        s_ref[...] = jnp.zeros_like(s_ref)

    row = lax.broadcasted_iota(jnp.int32, (CHUNK, CHUNK), 0)
    col = lax.broadcasted_iota(jnp.int32, (CHUNK, CHUNK), 1)
    lower = col <= row
    rxc = row ^ col
    cum_all = _masked_prefix_sum(lower.astype(BF16), la_ref[0])
    levels = (64, 32, 16, 8)
    for h in range(GLA_HEADS):
        ks = slice(h * GLA_DK, (h + 1) * GLA_DK)
        vs = slice(h * GLA_DV, (h + 1) * GLA_DV)
        q, k, cum = q_ref[0, :, ks], k_ref[0, :, ks], cum_all[:, ks]
        vb = v_ref[0, :, vs].astype(BF16)
        state = s_ref[0, h]
        scores = _intra_scores(q, k, cum, levels, rxc, lower)
        o = _dot(scores.astype(BF16), vb)
        o = o + _dot((q * jnp.exp(cum)).astype(BF16), state.astype(BF16))
        k_t, cum_t = k.T, cum.T
        last = cum_t[:, CHUNK - 1:CHUNK]
        kd_t = (k_t * jnp.exp(last - cum_t)).astype(BF16)
        s_ref[0, h] = jnp.exp(last) * state + _dot(kd_t, vb)
        og_ref[0, :, vs] = _out_norm_gate(o, gon_ref[...], gate_ref[0, :, vs])


def _gla_prompt(q, k, v, gate, la, gon):
    b, t, _ = q.shape
    blk = lambda w: pl.BlockSpec((1, CHUNK, w), lambda i, j: (i, j, 0))
    return pl.pallas_call(
        _gla_prompt_kernel,
        grid=(b, t // CHUNK),
        in_specs=[blk(GLA_KW), blk(GLA_KW), blk(GLA_VW), blk(GLA_VW), blk(GLA_KW),
                  _const_spec((1, GLA_DV))],
        out_specs=[blk(GLA_VW),
                   pl.BlockSpec((1, GLA_HEADS, GLA_DK, GLA_DV), lambda i, j: (i, 0, 0, 0))],
        out_shape=[jax.ShapeDtypeStruct((b, t, GLA_VW), BF16),
                   jax.ShapeDtypeStruct((b, GLA_HEADS, GLA_DK, GLA_DV), F32)],
        compiler_params=pltpu.CompilerParams(dimension_semantics=("arbitrary", "arbitrary"),
                                             vmem_limit_bytes=VMEM_LIMIT),
        name="gla_prompt",
    )(q, k, v, gate, la, gon)


def _gla_sample_kernel(nseq, tlen, q_ref, k_ref, v_ref, gate_ref, la_ref, gon_ref, s_in_ref,
                       og_ref, s_out_ref):
    r = nseq * tlen
    row = lax.broadcasted_iota(jnp.int32, (r, r), 0)
    col = lax.broadcasted_iota(jnp.int32, (r, r), 1)
    lower = col <= row
    rxc = row ^ col
    cum_all = _masked_prefix_sum(((rxc < tlen) & lower).astype(BF16), la_ref[...])
    seq_of_lane = lax.broadcasted_iota(jnp.int32, (GLA_DK, r), 1) >> 3
    for h in range(GLA_HEADS):
        ks = slice(h * GLA_DK, (h + 1) * GLA_DK)
        vs = slice(h * GLA_DV, (h + 1) * GLA_DV)
        q, k, cum = q_ref[:, ks], k_ref[:, ks], cum_all[:, ks]
        vb = v_ref[:, vs].astype(BF16)
        scores = _intra_scores(q, k, cum, (), rxc, lower)
        o_intra = _dot(scores.astype(BF16), vb)
        qin = (q * jnp.exp(cum)).astype(BF16)
        kd_t = (k * jnp.exp(_sublane_bcast(cum, tlen - 1, tlen) - cum)).T
        cum_t = cum.T
        for i in range(nseq):
            rs = slice(i * tlen, (i + 1) * tlen)
            state = s_in_ref[i, h]
            o = o_intra[rs] + _dot(qin[rs], state.astype(BF16))
            og_ref[rs, vs] = _out_norm_gate(o, gon_ref[...], gate_ref[rs, vs])
            last = cum_t[:, (i + 1) * tlen - 1:(i + 1) * tlen]
            kd_i = jnp.where(seq_of_lane == i, kd_t, 0.0).astype(BF16)
            s_out_ref[i, h] = jnp.exp(last) * state + _dot(kd_i, vb)


def _gla_sample(q, k, v, gate, la, gon, state, nseq, tlen):
    n = q.shape[0]
    r = nseq * tlen
    assert tlen == SUB
    row = lambda w: pl.BlockSpec((r, w), lambda i: (i, 0))
    st = pl.BlockSpec((nseq, GLA_HEADS, GLA_DK, GLA_DV), lambda i: (i, 0, 0, 0))
    return pl.pallas_call(
        functools.partial(_gla_sample_kernel, nseq, tlen),
        grid=(n // r,),
        in_specs=[row(GLA_KW), row(GLA_KW), row(GLA_VW), row(GLA_VW), row(GLA_KW),
                  _const_spec((1, GLA_DV)), st],
        out_specs=[row(GLA_VW), st],
        out_shape=[jax.ShapeDtypeStruct((n, GLA_VW), BF16),
                   jax.ShapeDtypeStruct(state.shape, F32)],
        compiler_params=pltpu.CompilerParams(dimension_semantics=("arbitrary",),
                                             vmem_limit_bytes=VMEM_LIMIT),
        name="gla_sample",
    )(q, k, v, gate, la, gon, state)


def _out_cmlp_kernel(mix_block, emit_v, x_ref, og_ref, wo_ref, g1_ref, wi_ref, lng_ref, lnb_ref,
                     ws_ref, bs_ref, wo2_ref, gf_ref, *out_refs):
    y_ref = out_refs[0]
    x1 = x_ref[...] + _dot(og_ref[...], wo_ref[...])
    hb = _rms(x1, g1_ref[...]).astype(BF16)
    w = CMLP_WIDTH
    u = _dot(hb, wi_ref[:, 0:w])
    v = _dot(hb, wi_ref[:, w:2 * w])
    gate = _dot(hb, wi_ref[:, 2 * w:3 * w])
    mu = jnp.mean(v, axis=-1, keepdims=True)
    vc = v - mu
    var = jnp.mean(vc * vc, axis=-1, keepdims=True)
    vn = vc * lax.rsqrt(var + EPS) * lng_ref[...] + lnb_ref[...]
    if emit_v:
        out_refs[1][...] = vn
    row = lax.broadcasted_iota(jnp.int32, (CHUNK, CHUNK), 0)
    col = lax.broadcasted_iota(jnp.int32, (CHUNK, CHUNK), 1)
    mix_mask = ((row ^ col) < mix_block) & (col <= row)
    ug = u * _silu(gate)
    for g in range(CMLP_GROUPS):
        gs = slice(g * CMLP_GROUP_DIM, (g + 1) * CMLP_GROUP_DIM)
        wsm = jnp.where(mix_mask, ws_ref[g], 0.0).astype(BF16)
        mixed = _dot(wsm, vn[:, gs].astype(BF16)) + bs_ref[:, gs]
        zg = (ug[:, gs] * mixed).astype(BF16)
        acc = _dot(zg, wo2_ref[gs, :])
        x1 = x1 + acc
    y_ref[...] = _rms(x1, gf_ref[...])


def _out_cmlp(x2d, og, wo, g1, wi, lng, lnb, ws_eff, bs_full, wo2, gf, mix_block, emit_v):
    n = x2d.shape[0]
    row = lambda w: pl.BlockSpec((CHUNK, w), lambda i: (i, 0))
    out_specs = [row(D_MODEL)]
    out_shape = [jax.ShapeDtypeStruct((n, D_MODEL), F32)]
    if emit_v:
        out_specs.append(row(CMLP_WIDTH))
        out_shape.append(jax.ShapeDtypeStruct((n, CMLP_WIDTH), F32))
    return pl.pallas_call(
        functools.partial(_out_cmlp_kernel, mix_block, emit_v),
        grid=(n // CHUNK,),
        in_specs=[row(D_MODEL), row(GLA_VW), _const_spec((GLA_VW, D_MODEL)),
                  _const_spec((1, D_MODEL)), _const_spec((D_MODEL, 3 * CMLP_WIDTH)),
                  _const_spec((1, CMLP_WIDTH)), _const_spec((1, CMLP_WIDTH)),
                  _const_spec((CMLP_GROUPS, CHUNK, CHUNK)), _const_spec((CHUNK, CMLP_WIDTH)),
                  _const_spec((CMLP_WIDTH, D_MODEL)), _const_spec((1, D_MODEL))],
        out_specs=out_specs,
        out_shape=out_shape,
        compiler_params=pltpu.CompilerParams(dimension_semantics=("arbitrary",),
                                             vmem_limit_bytes=VMEM_LIMIT),
        name="out_cmlp",
    )(x2d, og, wo, g1, wi, lng, lnb, ws_eff, bs_full, wo2, gf)


def kernel(x_prompt, x_sample, state_gla, norm_g, gla_w_in, gla_w_a_up, gla_b_a_up, gla_g_onorm,
           gla_w_out, cmlp_w_in, cmlp_ln_g, cmlp_ln_b, cmlp_w_spatial, cmlp_b_spatial, cmlp_w_out,
           norm_final):
    b, t, d = x_prompt.shape
    nb, nt, _ = x_sample.shape

    w_in_p = jnp.pad(gla_w_in[0], ((0, 0), (0, GLA_IN_PAD - gla_w_in.shape[2]))).astype(BF16)
    wa_p = jnp.pad(gla_w_a_up[0], ((0, 128 - GLA_GATE_RANK), (0, 0))).astype(BF16)
    ba = gla_b_a_up[0].reshape(1, GLA_KW)
    gon = gla_g_onorm[0].reshape(1, GLA_DV)
    wo = gla_w_out[0].astype(BF16)
    g0 = norm_g[0].reshape(1, d)
    g1 = norm_g[1].reshape(1, d)
    wi = cmlp_w_in[0].astype(BF16)
    lng = cmlp_ln_g[0].reshape(1, CMLP_WIDTH)
    lnb = cmlp_ln_b[0].reshape(1, CMLP_WIDTH)
    wo2 = cmlp_w_out[0].astype(BF16)
    gf = norm_final.reshape(1, d)
    ws = cmlp_w_spatial[0]
    bs = cmlp_b_spatial[0]

    def spatial_params(c):
        reps = CHUNK // c
        ws_eff = jnp.tile(ws[:, :c, :c], (1, reps, reps))
        bs_full = jnp.repeat(jnp.tile(bs[:, :c], (1, reps)).T, CMLP_GROUP_DIM, axis=1)
        return ws_eff, bs_full

    xp = x_prompt.reshape(b * t, d)
    q, k, v, gate, la = _gla_in(xp, g0, w_in_p, wa_p, ba, 512)
    r3 = lambda a: a.reshape(b, t, a.shape[-1])
    og, s_prompt = _gla_prompt(r3(q), r3(k), r3(v), r3(gate), r3(la), gon)
    ws_eff, bs_full = spatial_params(min(t, CHUNK))
    (y_prompt,) = _out_cmlp(xp, og.reshape(b * t, GLA_VW), wo, g1, wi, lng, lnb, ws_eff, bs_full,
                            wo2, gf, min(t, CHUNK), False)

    xs = x_sample.reshape(nb * nt, d)
    q, k, v, gate, la = _gla_in(xs, g0, w_in_p, wa_p, ba, 512)
    og, s_sample = _gla_sample(q, k, v, gate, la, gon, state_gla[0], 8, nt)
    ws_eff, bs_full = spatial_params(nt)
    y_sample, v_rows = _out_cmlp(xs, og, wo, g1, wi, lng, lnb, ws_eff, bs_full, wo2, gf, nt, True)

    return (y_prompt.reshape(b, t, d), y_sample.reshape(nb, nt, d), s_prompt[None],
            s_sample[None], v_rows.reshape(1, nb, nt, CMLP_WIDTH))
```

```python
import functools

import jax
import jax.numpy as jnp
from jax import lax
from jax.experimental import pallas as pl
from jax.experimental.pallas import tpu as pltpu

F32 = jnp.float32
BF16 = jnp.bfloat16

D_MODEL = 1024
GLA_HEADS = 4
GLA_DK = 128
GLA_DV = 256
GLA_KW = GLA_HEADS * GLA_DK
GLA_VW = GLA_HEADS * GLA_DV
GLA_GATE_RANK = 16
GLA_GATE_TAU = 16.0
CMLP_WIDTH = 1024
CMLP_GROUPS = 4
CMLP_GROUP_DIM = CMLP_WIDTH // CMLP_GROUPS
LANE = 128
CHUNK = 128
SUB = 8
EPS = 1e-6
VMEM_LIMIT = 56 * 1024 * 1024
DECAY_GUARD = 60.0

_Q0, _K0, _V0, _G0, _A0 = 0, GLA_KW, 2 * GLA_KW, 2 * GLA_KW + GLA_VW, 2 * GLA_KW + 2 * GLA_VW
GLA_IN_PAD = _A0 + LANE


def _pad_cols_odd_tiles(w):
    tiles = -(-w.shape[1] // LANE)
    tiles += 1 - tiles % 2
    return jnp.pad(w, ((0, 0), (0, tiles * LANE - w.shape[1]))).astype(BF16)


def _dot(a, b):
    return jnp.dot(a, b, preferred_element_type=F32)


def _dot_nt(a, b):
    return lax.dot_general(a, b, (((1,), (1,)), ((), ())), preferred_element_type=F32)


def _rms(x, g):
    return x * lax.rsqrt(jnp.mean(x * x, axis=-1, keepdims=True) + EPS) * g


def _silu(x):
    return x * jax.nn.sigmoid(x)


def _neg_abs(x):
    return lax.bitcast_convert_type(
        lax.bitcast_convert_type(x, jnp.uint32) | jnp.uint32(0x80000000), F32)


def _split3(x):
    hi = x.astype(BF16)
    r1 = x - hi.astype(F32)
    mid = r1.astype(BF16)
    lo = (r1 - mid.astype(F32)).astype(BF16)
    return hi, mid, lo


def _masked_prefix_sum(mask01, x):
    hi, mid, lo = _split3(x)
    return _dot(mask01, hi) + _dot(mask01, mid) + _dot(mask01, lo)


def _sublane_bcast(x, row, group):
    r, n = x.shape
    x3 = x.reshape(r // group, group, n)
    return jnp.broadcast_to(x3[:, row:row + 1, :], x3.shape).reshape(r, n)


def _index_masks(r):
    row = lax.broadcasted_iota(jnp.int32, (r, r), 0)
    col = lax.broadcasted_iota(jnp.int32, (r, r), 1)
    return row ^ col, col <= row


def _log_decay(a_low, wa_ref, ba_ref):
    z = _dot(a_low.astype(BF16), wa_ref[...]) + ba_ref[...]
    log_sig = jnp.minimum(z, 0.0) - jnp.log1p(jnp.exp(-jnp.abs(z)))
    return log_sig * (1.0 / GLA_GATE_TAU)


def _layernorm(v, g, b):
    vc = v - jnp.mean(v, axis=-1, keepdims=True)
    var = jnp.mean(vc * vc, axis=-1, keepdims=True)
    return vc * lax.rsqrt(var + EPS) * g + b


def _intra_scores_factored(q, k, cum, levels, rxc, lower):
    r = q.shape[0]
    parts = []
    for s in range(SUB):
        kb = _sublane_bcast(k, s, SUB)
        cb = _sublane_bcast(cum, s, SUB)
        parts.append((q * kb * jnp.exp(jnp.minimum(cum - cb, 0.0))).astype(BF16))
    acat = jnp.concatenate(parts, axis=1)
    pr = lax.broadcasted_iota(jnp.int32, (SUB * GLA_DK, r), 0)
    pc = lax.broadcasted_iota(jnp.int32, (SUB * GLA_DK, r), 1)
    lane_sum = ((pc & (SUB - 1)) == (pr >> 7)).astype(BF16)
    scores = jnp.where((rxc < SUB) & lower, _dot(acat, lane_sum), 0.0)
    for m in levels:
        qk = jnp.concatenate([(q if (i // m) % 2 else k)[i:i + m] for i in range(0, r, m)], axis=0)
        ref = _sublane_bcast(cum, m - 1, 2 * m)
        f = (qk * jnp.exp(_neg_abs(cum - ref))).astype(BF16)
        scores = jnp.where((rxc >= m) & (rxc < 2 * m) & lower, _dot_nt(f, f), scores)
    return scores


def _intra_scores_direct(qin, k, cum, mask):
    kout = (k * jnp.exp(-cum)).astype(BF16)
    return jnp.where(mask, _dot_nt(qin, kout), 0.0)


def _out_norm_gate(o, gon, gate):
    return (_rms(o, gon) * gate).astype(BF16)


def _gla_prompt_chunk(mild, rows, q_s, k_s, cum_s, v_s, gate_s, og_s, s_ref, gon_ref, rxc, lower):
    for h in range(GLA_HEADS):
        ks = slice(h * GLA_DK, (h + 1) * GLA_DK)
        vs = slice(h * GLA_DV, (h + 1) * GLA_DV)
        q, k, cum = q_s[rows, ks], k_s[rows, ks], cum_s[rows, ks]
        vb = v_s[rows, vs]
        state = s_ref[0, h]
        qin = (q * jnp.exp(cum)).astype(BF16)
        if mild:
            scores = _intra_scores_direct(qin, k, cum, lower)
        else:
            scores = _intra_scores_factored(q, k, cum, (64, 32, 16, 8), rxc, lower)
        o = _dot(scores.astype(BF16), vb) + _dot(qin, state.astype(BF16))
        k_t, cum_t = k.T, cum.T
        last = cum_t[:, CHUNK - 1:CHUNK]
        kd_t = (k_t * jnp.exp(last - cum_t)).astype(BF16)
        s_ref[0, h] = jnp.exp(last) * state + _dot(kd_t, vb)
        og_s[rows, vs] = _out_norm_gate(o, gon_ref[...], gate_s[rows, vs])


def _cmlp_tail(x1, nchunks, mix_block, g1_ref, wi_ref, lng_ref, lnb_ref, ws_ref, bs_ref, wo2_ref,
               gf_ref, v_out_ref):
    w = CMLP_WIDTH
    hb = _rms(x1, g1_ref[...]).astype(BF16)
    u = _dot(hb, wi_ref[:, 0:w])
    vn = _layernorm(_dot(hb, wi_ref[:, w:2 * w]), lng_ref[...], lnb_ref[...])
    ug = u * _silu(_dot(hb, wi_ref[:, 2 * w:3 * w]))
    if v_out_ref is not None:
        v_out_ref[...] = vn
    vnb = vn.astype(BF16)
    rxc, lower = _index_masks(CHUNK)
    mix_mask = (rxc < mix_block) & lower
    acc = x1
    for g in range(CMLP_GROUPS):
        gs = slice(g * CMLP_GROUP_DIM, (g + 1) * CMLP_GROUP_DIM)
        wsm = jnp.where(mix_mask, ws_ref[g], 0.0).astype(BF16)
        mixed = jnp.concatenate(
            [_dot(wsm, vnb[c * CHUNK:(c + 1) * CHUNK, gs]) + bs_ref[:, gs] for c in range(nchunks)],
            axis=0)
        acc = acc + _dot((ug[:, gs] * mixed).astype(BF16), wo2_ref[gs, 0:D_MODEL])
    return _rms(acc, gf_ref[...])


def _prompt_kernel(nchunks, x_ref, g0_ref, win_ref, wa_ref, ba_ref, gon_ref, wo_ref, g1_ref, wi_ref,
                   lng_ref, lnb_ref, ws_ref, bs_ref, wo2_ref, gf_ref, y_ref, s_ref,
                   q_s, k_s, cum_s, v_s, gate_s, og_s):
    @pl.when(pl.program_id(1) == 0)
    def _():
        s_ref[...] = jnp.zeros_like(s_ref)

    x = x_ref[0]
    hb = _rms(x, g0_ref[...]).astype(BF16)
    q_s[...] = _dot(hb, win_ref[:, _Q0:_K0]) * (GLA_DK ** -0.5)
    k_s[...] = _dot(hb, win_ref[:, _K0:_V0])
    v_s[...] = _dot(hb, win_ref[:, _V0:_G0]).astype(BF16)
    gate_s[...] = _silu(_dot(hb, win_ref[:, _G0:_A0]))
    la = _log_decay(_dot(hb, win_ref[:, _A0:GLA_IN_PAD]), wa_ref, ba_ref)
    rxc, lower = _index_masks(CHUNK)
    tri = lower.astype(BF16)
    total = None
    for c in range(nchunks):
        rows = slice(c * CHUNK, (c + 1) * CHUNK)
        cum = _masked_prefix_sum(tri, la[rows])
        cum_s[rows, :] = cum
        end = jnp.min(cum[CHUNK - SUB:CHUNK, :])
        total = end if total is None else jnp.minimum(total, end)
    mild = total > -DECAY_GUARD

    def recurrence(is_mild):
        for c in range(nchunks):
            _gla_prompt_chunk(is_mild, slice(c * CHUNK, (c + 1) * CHUNK), q_s, k_s, cum_s, v_s,
                              gate_s, og_s, s_ref, gon_ref, rxc, lower)

    pl.when(mild)(functools.partial(recurrence, True))
    pl.when(jnp.logical_not(mild))(functools.partial(recurrence, False))

    x1 = x + _dot(og_s[...], wo_ref[:, 0:D_MODEL])
    y_ref[0] = _cmlp_tail(x1, nchunks, CHUNK, g1_ref, wi_ref, lng_ref, lnb_ref, ws_ref, bs_ref,
                          wo2_ref, gf_ref, None)


def _const_spec(shape):
    nd = len(shape)
    return pl.BlockSpec(shape, lambda *_: (0,) * nd, pipeline_mode=pl.Buffered(1))


def _prompt_trunk(x, p, ws_eff, bs_full, nchunks):
    b, t, d = x.shape
    rb = nchunks * CHUNK
    consts = [p["g0"], p["w_in"], p["wa"], p["ba"], p["gon"], p["wo"], p["g1"], p["wi"], p["lng"],
              p["lnb"], ws_eff, bs_full, p["wo2"], p["gf"]]
    return pl.pallas_call(
        functools.partial(_prompt_kernel, nchunks),
        grid=(b, t // rb),
        in_specs=[pl.BlockSpec((1, rb, d), lambda i, j: (i, j, 0))]
        + [_const_spec(c.shape) for c in consts],
        out_specs=[pl.BlockSpec((1, rb, d), lambda i, j: (i, j, 0)),
                   pl.BlockSpec((1, GLA_HEADS, GLA_DK, GLA_DV), lambda i, j: (i, 0, 0, 0))],
        out_shape=[jax.ShapeDtypeStruct((b, t, d), F32),
                   jax.ShapeDtypeStruct((b, GLA_HEADS, GLA_DK, GLA_DV), F32)],
        scratch_shapes=[pltpu.VMEM((rb, GLA_KW), F32), pltpu.VMEM((rb, GLA_KW), F32),
                        pltpu.VMEM((rb, GLA_KW), F32), pltpu.VMEM((rb, GLA_VW), BF16),
                        pltpu.VMEM((rb, GLA_VW), F32), pltpu.VMEM((rb, GLA_VW), BF16)],
        compiler_params=pltpu.CompilerParams(dimension_semantics=("arbitrary", "arbitrary"),
                                             vmem_limit_bytes=VMEM_LIMIT),
        name="prompt_trunk",
    )(x, *consts)


def _gla_in_kernel(x_ref, g_ref, w_ref, wa_ref, ba_ref, q_ref, k_ref, v_ref, gate_ref, la_ref):
    hb = _rms(x_ref[...], g_ref[...]).astype(BF16)
    q_ref[...] = _dot(hb, w_ref[:, _Q0:_K0]) * (GLA_DK ** -0.5)
    k_ref[...] = _dot(hb, w_ref[:, _K0:_V0])
    v_ref[...] = _dot(hb, w_ref[:, _V0:_G0]).astype(BF16)
    gate_ref[...] = _silu(_dot(hb, w_ref[:, _G0:_A0]))
    la_ref[...] = _log_decay(_dot(hb, w_ref[:, _A0:GLA_IN_PAD]), wa_ref, ba_ref)


def _gla_in(x2d, p, tm):
    n = x2d.shape[0]
    row = lambda w: pl.BlockSpec((tm, w), lambda i: (i, 0))
    consts = [p["g0"], p["w_in"], p["wa"], p["ba"]]
    return pl.pallas_call(
        _gla_in_kernel,
        grid=(n // tm,),
        in_specs=[row(D_MODEL)] + [_const_spec(c.shape) for c in consts],
        out_specs=[row(GLA_KW), row(GLA_KW), row(GLA_VW), row(GLA_VW), row(GLA_KW)],
        out_shape=[jax.ShapeDtypeStruct((n, GLA_KW), F32), jax.ShapeDtypeStruct((n, GLA_KW), F32),
                   jax.ShapeDtypeStruct((n, GLA_VW), BF16), jax.ShapeDtypeStruct((n, GLA_VW), F32),
                   jax.ShapeDtypeStruct((n, GLA_KW), F32)],
        compiler_params=pltpu.CompilerParams(dimension_semantics=("arbitrary",),
                                             vmem_limit_bytes=VMEM_LIMIT),
        name="gla_in",
    )(x2d, *consts)


def _gla_sample_heads(mild, nseq, tlen, q_ref, k_ref, v_ref, gate_ref, cum_s, gon_ref, s_in_ref,
                      og_ref, s_out_ref, rxc, lower):
    r = nseq * tlen
    block_mask = (rxc < tlen) & lower
    seq_of_lane = lax.broadcasted_iota(jnp.int32, (GLA_DK, r), 1) >> 3
    for h in range(GLA_HEADS):
        ks = slice(h * GLA_DK, (h + 1) * GLA_DK)
        vs = slice(h * GLA_DV, (h + 1) * GLA_DV)
        q, k, cum = q_ref[:, ks], k_ref[:, ks], cum_s[:, ks]
        vb = v_ref[:, vs]
        qin = (q * jnp.exp(cum)).astype(BF16)
        if mild:
            scores = _intra_scores_direct(qin, k, cum, block_mask)
        else:
            scores = _intra_scores_factored(q, k, cum, (), rxc, lower)
        o_intra = _dot(scores.astype(BF16), vb)
        kd_t = (k * jnp.exp(_sublane_bcast(cum, tlen - 1, tlen) - cum)).T
        cum_t = cum.T
        for i in range(nseq):
            rs = slice(i * tlen, (i + 1) * tlen)
            state = s_in_ref[i, h]
            o = o_intra[rs] + _dot(qin[rs], state.astype(BF16))
            og_ref[rs, vs] = _out_norm_gate(o, gon_ref[...], gate_ref[rs, vs])
            last = cum_t[:, (i + 1) * tlen - 1:(i + 1) * tlen]
            kd_i = jnp.where(seq_of_lane == i, kd_t, 0.0).astype(BF16)
            s_out_ref[i, h] = jnp.exp(last) * state + _dot(kd_i, vb)


def _gla_sample_kernel(nseq, tlen, q_ref, k_ref, v_ref, gate_ref, la_ref, gon_ref, s_in_ref,
                       og_ref, s_out_ref, cum_s):
    rxc, lower = _index_masks(nseq * tlen)
    cum = _masked_prefix_sum(((rxc < tlen) & lower).astype(BF16), la_ref[...])
    cum_s[...] = cum
    mild = jnp.min(cum) > -DECAY_GUARD
    args = (nseq, tlen, q_ref, k_ref, v_ref, gate_ref, cum_s, gon_ref, s_in_ref, og_ref, s_out_ref,
            rxc, lower)
    pl.when(mild)(functools.partial(_gla_sample_heads, True, *args))
    pl.when(jnp.logical_not(mild))(functools.partial(_gla_sample_heads, False, *args))


def _gla_sample(q, k, v, gate, la, gon, state, nseq, tlen):
    n = q.shape[0]
    r = nseq * tlen
    assert tlen == SUB
    row = lambda w: pl.BlockSpec((r, w), lambda i: (i, 0))
    st = pl.BlockSpec((nseq, GLA_HEADS, GLA_DK, GLA_DV), lambda i: (i, 0, 0, 0))
    return pl.pallas_call(
        functools.partial(_gla_sample_kernel, nseq, tlen),
        grid=(n // r,),
        in_specs=[row(GLA_KW), row(GLA_KW), row(GLA_VW), row(GLA_VW), row(GLA_KW),
                  _const_spec((1, GLA_DV)), st],
        out_specs=[row(GLA_VW), st],
        out_shape=[jax.ShapeDtypeStruct((n, GLA_VW), BF16),
                   jax.ShapeDtypeStruct(state.shape, F32)],
        scratch_shapes=[pltpu.VMEM((r, GLA_KW), F32)],
        compiler_params=pltpu.CompilerParams(dimension_semantics=("arbitrary",),
                                             vmem_limit_bytes=VMEM_LIMIT),
        name="gla_sample",
    )(q, k, v, gate, la, gon, state)


def _out_cmlp_kernel(mix_block, x_ref, og_ref, wo_ref, g1_ref, wi_ref, lng_ref, lnb_ref, ws_ref,
                     bs_ref, wo2_ref, gf_ref, y_ref, v_out_ref):
    x1 = x_ref[...] + _dot(og_ref[...], wo_ref[:, 0:D_MODEL])
    y_ref[...] = _cmlp_tail(x1, 1, mix_block, g1_ref, wi_ref, lng_ref, lnb_ref, ws_ref, bs_ref,
                            wo2_ref, gf_ref, v_out_ref)


def _out_cmlp(x2d, og, p, ws_eff, bs_full, mix_block):
    n = x2d.shape[0]
    row = lambda w: pl.BlockSpec((CHUNK, w), lambda i: (i, 0))
    consts = [p["wo"], p["g1"], p["wi"], p["lng"], p["lnb"], ws_eff, bs_full, p["wo2"], p["gf"]]
    return pl.pallas_call(
        functools.partial(_out_cmlp_kernel, mix_block),
        grid=(n // CHUNK,),
        in_specs=[row(D_MODEL), row(GLA_VW)] + [_const_spec(c.shape) for c in consts],
        out_specs=[row(D_MODEL), row(CMLP_WIDTH)],
        out_shape=[jax.ShapeDtypeStruct((n, D_MODEL), F32),
                   jax.ShapeDtypeStruct((n, CMLP_WIDTH), F32)],
        compiler_params=pltpu.CompilerParams(dimension_semantics=("arbitrary",),
                                             vmem_limit_bytes=VMEM_LIMIT),
        name="out_cmlp",
    )(x2d, og, *consts)


def kernel(x_prompt, x_sample, state_gla, norm_g, gla_w_in, gla_w_a_up, gla_b_a_up, gla_g_onorm,
           gla_w_out, cmlp_w_in, cmlp_ln_g, cmlp_ln_b, cmlp_w_spatial, cmlp_b_spatial, cmlp_w_out,
           norm_final):
    b, t, d = x_prompt.shape
    nb, nt, _ = x_sample.shape
    assert t % (2 * CHUNK) == 0 and (nb * nt) % CHUNK == 0 and CHUNK % nt == 0

    p = {
        "g0": norm_g[0].reshape(1, d),
        "g1": norm_g[1].reshape(1, d),
        "w_in": jnp.pad(gla_w_in[0], ((0, 0), (0, GLA_IN_PAD - gla_w_in.shape[2]))).astype(BF16),
        "wa": jnp.pad(gla_w_a_up[0], ((0, LANE - GLA_GATE_RANK), (0, 0))).astype(BF16),
        "ba": gla_b_a_up[0].reshape(1, GLA_KW),
        "gon": gla_g_onorm[0].reshape(1, GLA_DV),
        "wo": _pad_cols_odd_tiles(gla_w_out[0]),
        "wi": _pad_cols_odd_tiles(cmlp_w_in[0]),
        "lng": cmlp_ln_g[0].reshape(1, CMLP_WIDTH),
        "lnb": cmlp_ln_b[0].reshape(1, CMLP_WIDTH),
        "wo2": _pad_cols_odd_tiles(cmlp_w_out[0]),
        "gf": norm_final.reshape(1, d),
    }
    assert p["w_in"].shape[1] // LANE % 2 == 1
    ws = cmlp_w_spatial[0]
    bs = cmlp_b_spatial[0]

    def spatial_params(c):
        reps = CHUNK // c
        ws_eff = jnp.tile(ws[:, :c, :c], (1, reps, reps))
        bs_full = jnp.repeat(jnp.tile(bs[:, :c], (1, reps)).T, CMLP_GROUP_DIM, axis=1)
        return ws_eff, bs_full

    y_prompt, s_prompt = _prompt_trunk(x_prompt, p, *spatial_params(CHUNK), 2)

    xs = x_sample.reshape(nb * nt, d)
    q, k, v, gate, la = _gla_in(xs, p, 512)
    og, s_sample = _gla_sample(q, k, v, gate, la, p["gon"], state_gla[0], 8, nt)
    y_sample, v_rows = _out_cmlp(xs, og, p, *spatial_params(nt), nt)

    return (y_prompt, y_sample.reshape(nb, nt, d), s_prompt[None], s_sample[None],
            v_rows.reshape(1, nb, nt, CMLP_WIDTH))
```

```python
import functools

import jax
import jax.numpy as jnp
from jax import lax
from jax.experimental import pallas as pl
from jax.experimental.pallas import tpu as pltpu

F32 = jnp.float32
BF16 = jnp.bfloat16

D_MODEL = 1024
GLA_HEADS = 4
GLA_DK = 128
GLA_DV = 256
GLA_KW = GLA_HEADS * GLA_DK
GLA_VW = GLA_HEADS * GLA_DV
GLA_GATE_RANK = 16
GLA_GATE_TAU = 16.0
CMLP_WIDTH = 1024
CMLP_GROUPS = 4
CMLP_GROUP_DIM = CMLP_WIDTH // CMLP_GROUPS
LANE = 128
CHUNK = 128
SUB = 8
EPS = 1e-6
VMEM_LIMIT = 56 * 1024 * 1024
DECAY_GUARD = 60.0

_Q0, _K0, _V0, _G0, _A0 = 0, GLA_KW, 2 * GLA_KW, 2 * GLA_KW + GLA_VW, 2 * GLA_KW + 2 * GLA_VW
GLA_IN_PAD = _A0 + LANE


def _pad_cols_odd_tiles(w):
    tiles = -(-w.shape[1] // LANE)
    tiles += 1 - tiles % 2
    return jnp.pad(w, ((0, 0), (0, tiles * LANE - w.shape[1]))).astype(BF16)


def _dot(a, b):
    return jnp.dot(a, b, preferred_element_type=F32)


def _dot_nt(a, b):
    return lax.dot_general(a, b, (((1,), (1,)), ((), ())), preferred_element_type=F32)


def _rms(x, g):
    return x * lax.rsqrt(jnp.mean(x * x, axis=-1, keepdims=True) + EPS) * g


def _silu(x):
    return x * jax.nn.sigmoid(x)


def _neg_abs(x):
    return lax.bitcast_convert_type(
        lax.bitcast_convert_type(x, jnp.uint32) | jnp.uint32(0x80000000), F32)


def _split3(x):
    hi = x.astype(BF16)
    r1 = x - hi.astype(F32)
    mid = r1.astype(BF16)
    lo = (r1 - mid.astype(F32)).astype(BF16)
    return hi, mid, lo


def _masked_prefix_sum(mask01, x):
    hi, mid, lo = _split3(x)
    return _dot(mask01, hi) + _dot(mask01, mid) + _dot(mask01, lo)


def _sublane_bcast(x, row, group):
    r, n = x.shape
    x3 = x.reshape(r // group, group, n)
    return jnp.broadcast_to(x3[:, row:row + 1, :], x3.shape).reshape(r, n)


def _index_masks(r):
    row = lax.broadcasted_iota(jnp.int32, (r, r), 0)
    col = lax.broadcasted_iota(jnp.int32, (r, r), 1)
    return row ^ col, col <= row


def _log_decay(a_low, wa_ref, ba_ref):
    z = _dot(a_low.astype(BF16), wa_ref[...]) + ba_ref[...]
    log_sig = jnp.minimum(z, 0.0) - jnp.log1p(jnp.exp(-jnp.abs(z)))
    return log_sig * (1.0 / GLA_GATE_TAU)


def _layernorm(v, g, b):
    vc = v - jnp.mean(v, axis=-1, keepdims=True)
    var = jnp.mean(vc * vc, axis=-1, keepdims=True)
    return vc * lax.rsqrt(var + EPS) * g + b


def _intra_scores_factored(q, k, cum, levels, rxc, lower):
    r = q.shape[0]
    parts = []
    for s in range(SUB):
        kb = _sublane_bcast(k, s, SUB)
        cb = _sublane_bcast(cum, s, SUB)
        parts.append((q * kb * jnp.exp(jnp.minimum(cum - cb, 0.0))).astype(BF16))
    acat = jnp.concatenate(parts, axis=1)
    pr = lax.broadcasted_iota(jnp.int32, (SUB * GLA_DK, r), 0)
    pc = lax.broadcasted_iota(jnp.int32, (SUB * GLA_DK, r), 1)
    lane_sum = ((pc & (SUB - 1)) == (pr >> 7)).astype(BF16)
    scores = jnp.where((rxc < SUB) & lower, _dot(acat, lane_sum), 0.0)
    for m in levels:
        qk = jnp.concatenate([(q if (i // m) % 2 else k)[i:i + m] for i in range(0, r, m)], axis=0)
        ref = _sublane_bcast(cum, m - 1, 2 * m)
        f = (qk * jnp.exp(_neg_abs(cum - ref))).astype(BF16)
        scores = jnp.where((rxc >= m) & (rxc < 2 * m) & lower, _dot_nt(f, f), scores)
    return scores


def _intra_scores_direct(qin, k, cum, mask):
    kout = (k * jnp.exp(-cum)).astype(BF16)
    return jnp.where(mask, _dot_nt(qin, kout), 0.0)


def _out_norm_gate(o, gon, gate):
    return (_rms(o, gon) * gate).astype(BF16)


def _gla_prompt_chunk(mild, rows, q_s, k_s, cum_s, v_s, gate_s, og_ref, s_in, s_out, gon_ref, rxc,
                      lower):
    for h in range(GLA_HEADS):
        ks = slice(h * GLA_DK, (h + 1) * GLA_DK)
        vs = slice(h * GLA_DV, (h + 1) * GLA_DV)
        q, k, cum = q_s[rows, ks], k_s[rows, ks], cum_s[rows, ks]
        vb = v_s[rows, vs]
        state = s_in[h]
        qin = (q * jnp.exp(cum)).astype(BF16)
        if mild:
            scores = _intra_scores_direct(qin, k, cum, lower)
        else:
            scores = _intra_scores_factored(q, k, cum, (64, 32, 16, 8), rxc, lower)
        o = _dot(scores.astype(BF16), vb) + _dot(qin, state.astype(BF16))
        k_t, cum_t = k.T, cum.T
        last = cum_t[:, CHUNK - 1:CHUNK]
        kd_t = (k_t * jnp.exp(last - cum_t)).astype(BF16)
        s_out[h] = jnp.exp(last) * state + _dot(kd_t, vb)
        og_ref[rows, vs] = _out_norm_gate(o, gon_ref[...], gate_s[rows, vs])


def _cmlp_in(x1, g1_ref, wi_ref, lng_ref, lnb_ref):
    w = CMLP_WIDTH
    hb = _rms(x1, g1_ref[...]).astype(BF16)
    u = _dot(hb, wi_ref[:, 0:w])
    vn = _layernorm(_dot(hb, wi_ref[:, w:2 * w]), lng_ref[...], lnb_ref[...])
    ug = u * _silu(_dot(hb, wi_ref[:, 2 * w:3 * w]))
    return ug, vn


def _cmlp_out(x1, ug, vn, nchunks, mix_block, ws_ref, bs_ref, wo2_ref, gf_ref):
    vnb = vn.astype(BF16)
    rxc, lower = _index_masks(CHUNK)
    mix_mask = (rxc < mix_block) & lower
    acc = x1
    for g in range(CMLP_GROUPS):
        gs = slice(g * CMLP_GROUP_DIM, (g + 1) * CMLP_GROUP_DIM)
        wsm = jnp.where(mix_mask, ws_ref[g], 0.0).astype(BF16)
        mixed = jnp.concatenate(
            [_dot(wsm, vnb[c * CHUNK:(c + 1) * CHUNK, gs]) + bs_ref[:, gs] for c in range(nchunks)],
            axis=0)
        acc = acc + _dot((ug[:, gs] * mixed).astype(BF16), wo2_ref[gs, 0:D_MODEL])
    return _rms(acc, gf_ref[...])


def _prompt_kernel(nchunks, nblk, blk_per_seq, xa_ref, xc_ref, g0_ref, win_ref, wa_ref, ba_ref,
                   gon_ref, wo_ref, g1_ref, wi_ref, lng_ref, lnb_ref, ws_ref, bs_ref, wo2_ref, gf_ref,
                   y_ref, s_ref, q_s, k_s, cum_s, v_s, gate_s, og_s, st_s):
    j = pl.program_id(0)
    valid = j < nblk
    jj = jnp.minimum(j, nblk - 1)
    is_first = lax.rem(jj, blk_per_seq) == 0
    slot = lax.rem(j, 2)
    og_cur, og_prev = og_s.at[slot], og_s.at[1 - slot]

    @pl.when(j == 0)
    def _():
        og_s[...] = jnp.zeros_like(og_s)

    @pl.when(is_first)
    def _():
        s_ref[...] = jnp.zeros_like(s_ref)

    hb = _rms(xa_ref[0], g0_ref[...]).astype(BF16)
    og_prev_val = og_prev[...]
    a_low = _dot(hb, win_ref[:, _A0:GLA_IN_PAD])
    q_s[...] = _dot(hb, win_ref[:, _Q0:_K0]) * (GLA_DK ** -0.5)
    k_s[...] = _dot(hb, win_ref[:, _K0:_V0])
    la = _log_decay(a_low, wa_ref, ba_ref)
    v_s[...] = _dot(hb, win_ref[:, _V0:_G0]).astype(BF16)
    gate_lin = _dot(hb, win_ref[:, _G0:_A0])
    x1 = xc_ref[0] + _dot(og_prev_val, wo_ref[:, 0:D_MODEL])
    gate_s[...] = _silu(gate_lin)
    rxc, lower = _index_masks(CHUNK)
    tri = lower.astype(BF16)
    total = None
    for c in range(nchunks):
        rows = slice(c * CHUNK, (c + 1) * CHUNK)
        cum = _masked_prefix_sum(tri, la[rows])
        cum_s[rows, :] = cum
        end = jnp.min(cum[CHUNK - SUB:CHUNK, :])
        total = end if total is None else jnp.minimum(total, end)
    mild = total > -DECAY_GUARD

    ug = vn = None
    for c in range(nchunks):
        _gla_prompt_chunk(True, slice(c * CHUNK, (c + 1) * CHUNK), q_s, k_s, cum_s, v_s, gate_s,
                          og_cur, s_ref.at[0] if c == 0 else st_s, st_s, gon_ref, rxc, lower)
        if c == 0:
            ug, vn = _cmlp_in(x1, g1_ref, wi_ref, lng_ref, lnb_ref)
    y_ref[0] = _cmlp_out(x1, ug, vn, nchunks, CHUNK, ws_ref, bs_ref, wo2_ref, gf_ref)
    commit = jnp.logical_and(valid, mild)
    for h in range(GLA_HEADS):
        s_ref[0, h] = jnp.where(commit, st_s[h], s_ref[0, h])

    @pl.when(jnp.logical_and(valid, jnp.logical_not(mild)))
    def _():
        for c in range(nchunks):
            _gla_prompt_chunk(False, slice(c * CHUNK, (c + 1) * CHUNK), q_s, k_s, cum_s, v_s, gate_s,
                              og_cur, s_ref.at[0], s_ref.at[0], gon_ref, rxc, lower)


def _const_spec(shape):
    nd = len(shape)
    return pl.BlockSpec(shape, lambda *_: (0,) * nd, pipeline_mode=pl.Buffered(1))


def _prompt_trunk(x, p, ws_eff, bs_full, nchunks):
    b, t, d = x.shape
    rb = nchunks * CHUNK
    bps = t // rb
    nblk = b * bps
    consts = [p["g0"], p["w_in"], p["wa"], p["ba"], p["gon"], p["wo"], p["g1"], p["wi"], p["lng"],
              p["lnb"], ws_eff, bs_full, p["wo2"], p["gf"]]

    def cur(j):
        jj = jnp.minimum(j, nblk - 1)
        return jj // bps, jj % bps

    def prev(j):
        jj = jnp.maximum(j - 1, 0)
        return jj // bps, jj % bps

    return pl.pallas_call(
        functools.partial(_prompt_kernel, nchunks, nblk, bps),
        grid=(nblk + 1,),
        in_specs=[pl.BlockSpec((1, rb, d), lambda j: (*cur(j), 0)),
                  pl.BlockSpec((1, rb, d), lambda j: (*prev(j), 0))]
        + [_const_spec(c.shape) for c in consts],
        out_specs=[pl.BlockSpec((1, rb, d), lambda j: (*prev(j), 0)),
                   pl.BlockSpec((1, GLA_HEADS, GLA_DK, GLA_DV), lambda j: (cur(j)[0], 0, 0, 0))],
        out_shape=[jax.ShapeDtypeStruct((b, t, d), F32),
                   jax.ShapeDtypeStruct((b, GLA_HEADS, GLA_DK, GLA_DV), F32)],
        scratch_shapes=[pltpu.VMEM((rb, GLA_KW), F32), pltpu.VMEM((rb, GLA_KW), F32),
                        pltpu.VMEM((rb, GLA_KW), F32), pltpu.VMEM((rb, GLA_VW), BF16),
                        pltpu.VMEM((rb, GLA_VW), F32), pltpu.VMEM((2, rb, GLA_VW), BF16),
                        pltpu.VMEM((GLA_HEADS, GLA_DK, GLA_DV), F32)],
        compiler_params=pltpu.CompilerParams(dimension_semantics=("arbitrary",),
                                             vmem_limit_bytes=VMEM_LIMIT),
        name="prompt_trunk",
    )(x, x, *consts)


def _gla_in_kernel(x_ref, g_ref, w_ref, wa_ref, ba_ref, q_ref, k_ref, v_ref, gate_ref, la_ref):
    hb = _rms(x_ref[...], g_ref[...]).astype(BF16)
    q_ref[...] = _dot(hb, w_ref[:, _Q0:_K0]) * (GLA_DK ** -0.5)
    k_ref[...] = _dot(hb, w_ref[:, _K0:_V0])
    v_ref[...] = _dot(hb, w_ref[:, _V0:_G0]).astype(BF16)
    gate_ref[...] = _silu(_dot(hb, w_ref[:, _G0:_A0]))
    la_ref[...] = _log_decay(_dot(hb, w_ref[:, _A0:GLA_IN_PAD]), wa_ref, ba_ref)


def _gla_in(x2d, p, tm):
    n = x2d.shape[0]
    row = lambda w: pl.BlockSpec((tm, w), lambda i: (i, 0))
    consts = [p["g0"], p["w_in"], p["wa"], p["ba"]]
    return pl.pallas_call(
        _gla_in_kernel,
        grid=(n // tm,),
        in_specs=[row(D_MODEL)] + [_const_spec(c.shape) for c in consts],
        out_specs=[row(GLA_KW), row(GLA_KW), row(GLA_VW), row(GLA_VW), row(GLA_KW)],
        out_shape=[jax.ShapeDtypeStruct((n, GLA_KW), F32), jax.ShapeDtypeStruct((n, GLA_KW), F32),
                   jax.ShapeDtypeStruct((n, GLA_VW), BF16), jax.ShapeDtypeStruct((n, GLA_VW), F32),
                   jax.ShapeDtypeStruct((n, GLA_KW), F32)],
        compiler_params=pltpu.CompilerParams(dimension_semantics=("arbitrary",),
                                             vmem_limit_bytes=VMEM_LIMIT),
        name="gla_in",
    )(x2d, *consts)


def _gla_sample_heads(mild, nseq, tlen, q_ref, k_ref, v_ref, gate_ref, cum_s, gon_ref, s_in_ref,
                      og_ref, s_out_ref, rxc, lower):
    r = nseq * tlen
    block_mask = (rxc < tlen) & lower
    seq_of_lane = lax.broadcasted_iota(jnp.int32, (GLA_DK, r), 1) >> 3
    for h in range(GLA_HEADS):
        ks = slice(h * GLA_DK, (h + 1) * GLA_DK)
        vs = slice(h * GLA_DV, (h + 1) * GLA_DV)
        q, k, cum = q_ref[:, ks], k_ref[:, ks], cum_s[:, ks]
        vb = v_ref[:, vs]
        qin = (q * jnp.exp(cum)).astype(BF16)
        if mild:
            scores = _intra_scores_direct(qin, k, cum, block_mask)
        else:
            scores = _intra_scores_factored(q, k, cum, (), rxc, lower)
        o_intra = _dot(scores.astype(BF16), vb)
        kd_t = (k * jnp.exp(_sublane_bcast(cum, tlen - 1, tlen) - cum)).T
        cum_t = cum.T
        for i in range(nseq):
            rs = slice(i * tlen, (i + 1) * tlen)
            state = s_in_ref[i, h]
            o = o_intra[rs] + _dot(qin[rs], state.astype(BF16))
            og_ref[rs, vs] = _out_norm_gate(o, gon_ref[...], gate_ref[rs, vs])
            last = cum_t[:, (i + 1) * tlen - 1:(i + 1) * tlen]
            kd_i = jnp.where(seq_of_lane == i, kd_t, 0.0).astype(BF16)
            s_out_ref[i, h] = jnp.exp(last) * state + _dot(kd_i, vb)


def _gla_sample_kernel(nseq, tlen, q_ref, k_ref, v_ref, gate_ref, la_ref, gon_ref, s_in_ref,
                       og_ref, s_out_ref, cum_s):
    rxc, lower = _index_masks(nseq * tlen)
    cum = _masked_prefix_sum(((rxc < tlen) & lower).astype(BF16), la_ref[...])
    cum_s[...] = cum
    mild = jnp.min(cum) > -DECAY_GUARD
    args = (nseq, tlen, q_ref, k_ref, v_ref, gate_ref, cum_s, gon_ref, s_in_ref, og_ref, s_out_ref,
            rxc, lower)
    pl.when(mild)(functools.partial(_gla_sample_heads, True, *args))
    pl.when(jnp.logical_not(mild))(functools.partial(_gla_sample_heads, False, *args))


def _gla_sample(q, k, v, gate, la, gon, state, nseq, tlen):
    n = q.shape[0]
    r = nseq * tlen
    assert tlen == SUB
    row = lambda w: pl.BlockSpec((r, w), lambda i: (i, 0))
    st = pl.BlockSpec((nseq, GLA_HEADS, GLA_DK, GLA_DV), lambda i: (i, 0, 0, 0))
    return pl.pallas_call(
        functools.partial(_gla_sample_kernel, nseq, tlen),
        grid=(n // r,),
        in_specs=[row(GLA_KW), row(GLA_KW), row(GLA_VW), row(GLA_VW), row(GLA_KW),
                  _const_spec((1, GLA_DV)), st],
        out_specs=[row(GLA_VW), st],
        out_shape=[jax.ShapeDtypeStruct((n, GLA_VW), BF16),
                   jax.ShapeDtypeStruct(state.shape, F32)],
        scratch_shapes=[pltpu.VMEM((r, GLA_KW), F32)],
        compiler_params=pltpu.CompilerParams(dimension_semantics=("arbitrary",),
                                             vmem_limit_bytes=VMEM_LIMIT),
        name="gla_sample",
    )(q, k, v, gate, la, gon, state)


def _out_cmlp_kernel(mix_block, x_ref, og_ref, wo_ref, g1_ref, wi_ref, lng_ref, lnb_ref, ws_ref,
                     bs_ref, wo2_ref, gf_ref, y_ref, v_out_ref):
    x1 = x_ref[...] + _dot(og_ref[...], wo_ref[:, 0:D_MODEL])
    ug, vn = _cmlp_in(x1, g1_ref, wi_ref, lng_ref, lnb_ref)
    v_out_ref[...] = vn
    y_ref[...] = _cmlp_out(x1, ug, vn, 1, mix_block, ws_ref, bs_ref, wo2_ref, gf_ref)


def _out_cmlp(x2d, og, p, ws_eff, bs_full, mix_block):
    n = x2d.shape[0]
    row = lambda w: pl.BlockSpec((CHUNK, w), lambda i: (i, 0))
    consts = [p["wo"], p["g1"], p["wi"], p["lng"], p["lnb"], ws_eff, bs_full, p["wo2"], p["gf"]]
    return pl.pallas_call(
        functools.partial(_out_cmlp_kernel, mix_block),
        grid=(n // CHUNK,),
        in_specs=[row(D_MODEL), row(GLA_VW)] + [_const_spec(c.shape) for c in consts],
        out_specs=[row(D_MODEL), row(CMLP_WIDTH)],
        out_shape=[jax.ShapeDtypeStruct((n, D_MODEL), F32),
                   jax.ShapeDtypeStruct((n, CMLP_WIDTH), F32)],
        compiler_params=pltpu.CompilerParams(dimension_semantics=("arbitrary",),
                                             vmem_limit_bytes=VMEM_LIMIT),
        name="out_cmlp",
    )(x2d, og, *consts)


def kernel(x_prompt, x_sample, state_gla, norm_g, gla_w_in, gla_w_a_up, gla_b_a_up, gla_g_onorm,
           gla_w_out, cmlp_w_in, cmlp_ln_g, cmlp_ln_b, cmlp_w_spatial, cmlp_b_spatial, cmlp_w_out,
           norm_final):
    b, t, d = x_prompt.shape
    nb, nt, _ = x_sample.shape
    assert t % (2 * CHUNK) == 0 and (nb * nt) % CHUNK == 0 and CHUNK % nt == 0

    p = {
        "g0": norm_g[0].reshape(1, d),
        "g1": norm_g[1].reshape(1, d),
        "w_in": jnp.pad(gla_w_in[0], ((0, 0), (0, GLA_IN_PAD - gla_w_in.shape[2]))).astype(BF16),
        "wa": jnp.pad(gla_w_a_up[0], ((0, LANE - GLA_GATE_RANK), (0, 0))).astype(BF16),
        "ba": gla_b_a_up[0].reshape(1, GLA_KW),
        "gon": gla_g_onorm[0].reshape(1, GLA_DV),
        "wo": _pad_cols_odd_tiles(gla_w_out[0]),
        "wi": _pad_cols_odd_tiles(cmlp_w_in[0]),
        "lng": cmlp_ln_g[0].reshape(1, CMLP_WIDTH),
        "lnb": cmlp_ln_b[0].reshape(1, CMLP_WIDTH),
        "wo2": _pad_cols_odd_tiles(cmlp_w_out[0]),
        "gf": norm_final.reshape(1, d),
    }
    assert p["w_in"].shape[1] // LANE % 2 == 1
    ws = cmlp_w_spatial[0]
    bs = cmlp_b_spatial[0]

    def spatial_params(c):
        reps = CHUNK // c
        ws_eff = jnp.tile(ws[:, :c, :c], (1, reps, reps))
        bs_full = jnp.repeat(jnp.tile(bs[:, :c], (1, reps)).T, CMLP_GROUP_DIM, axis=1)
        return ws_eff, bs_full

    y_prompt, s_prompt = _prompt_trunk(x_prompt, p, *spatial_params(CHUNK), 2)

    xs = x_sample.reshape(nb * nt, d)
    q, k, v, gate, la = _gla_in(xs, p, 512)
    og, s_sample = _gla_sample(q, k, v, gate, la, p["gon"], state_gla[0], 8, nt)
    y_sample, v_rows = _out_cmlp(xs, og, p, *spatial_params(nt), nt)

    return (y_prompt, y_sample.reshape(nb, nt, d), s_prompt[None], s_sample[None],
            v_rows.reshape(1, nb, nt, CMLP_WIDTH))
```

```python
import functools

import jax
import jax.numpy as jnp
from jax import lax
from jax.experimental import pallas as pl
from jax.experimental.pallas import tpu as pltpu

F32 = jnp.float32
BF16 = jnp.bfloat16

D_MODEL = 1024
GLA_HEADS = 4
GLA_DK = 128
GLA_DV = 256
GLA_KW = GLA_HEADS * GLA_DK
GLA_VW = GLA_HEADS * GLA_DV
GLA_GATE_RANK = 16
GLA_GATE_TAU = 16.0
CMLP_WIDTH = 1024
CMLP_GROUPS = 4
CMLP_GROUP_DIM = CMLP_WIDTH // CMLP_GROUPS
LANE = 128
CHUNK = 128
SUB = 8
EPS = 1e-6
VMEM_LIMIT = 56 * 1024 * 1024
DECAY_GUARD = 60.0

_Q0, _K0, _V0, _G0, _A0 = 0, GLA_KW, 2 * GLA_KW, 2 * GLA_KW + GLA_VW, 2 * GLA_KW + 2 * GLA_VW
GLA_IN_PAD = _A0 + LANE


def _pad_cols_odd_tiles(w):
    tiles = -(-w.shape[1] // LANE)
    tiles += 1 - tiles % 2
    return jnp.pad(w, ((0, 0), (0, tiles * LANE - w.shape[1]))).astype(BF16)


def _dot(a, b):
    return jnp.dot(a, b, preferred_element_type=F32)


def _dot_nt(a, b):
    return lax.dot_general(a, b, (((1,), (1,)), ((), ())), preferred_element_type=F32)


def _rms(x, g):
    return x * lax.rsqrt(jnp.mean(x * x, axis=-1, keepdims=True) + EPS) * g


def _silu(x):
    return x * jax.nn.sigmoid(x)


def _neg_abs(x):
    return lax.bitcast_convert_type(
        lax.bitcast_convert_type(x, jnp.uint32) | jnp.uint32(0x80000000), F32)


def _split3(x):
    hi = x.astype(BF16)
    r1 = x - hi.astype(F32)
    mid = r1.astype(BF16)
    lo = (r1 - mid.astype(F32)).astype(BF16)
    return hi, mid, lo


def _masked_prefix_sum(mask01, x):
    hi, mid, lo = _split3(x)
    return (_dot(jnp.concatenate([mask01, mask01], axis=1), jnp.concatenate([hi, mid], axis=0))
            + _dot(mask01, lo))


def _sublane_bcast(x, row, group):
    r, n = x.shape
    x3 = x.reshape(r // group, group, n)
    return jnp.broadcast_to(x3[:, row:row + 1, :], x3.shape).reshape(r, n)


def _index_masks(r):
    row = lax.broadcasted_iota(jnp.int32, (r, r), 0)
    col = lax.broadcasted_iota(jnp.int32, (r, r), 1)
    return row ^ col, col <= row


def _log_decay(a_low, wa_ref, ba_ref):
    z = _dot(a_low.astype(BF16), wa_ref[...]) + ba_ref[...]
    log_sig = jnp.minimum(z, 0.0) - jnp.log1p(jnp.exp(-jnp.abs(z)))
    return log_sig * (1.0 / GLA_GATE_TAU)


def _layernorm(v, g, b):
    vc = v - jnp.mean(v, axis=-1, keepdims=True)
    var = jnp.mean(vc * vc, axis=-1, keepdims=True)
    return vc * lax.rsqrt(var + EPS) * g + b


def _intra_scores_factored(q, k, cum, levels, rxc, lower):
    r = q.shape[0]
    parts = []
    for s in range(SUB):
        kb = _sublane_bcast(k, s, SUB)
        cb = _sublane_bcast(cum, s, SUB)
        parts.append((q * kb * jnp.exp(jnp.minimum(cum - cb, 0.0))).astype(BF16))
    acat = jnp.concatenate(parts, axis=1)
    pr = lax.broadcasted_iota(jnp.int32, (SUB * GLA_DK, r), 0)
    pc = lax.broadcasted_iota(jnp.int32, (SUB * GLA_DK, r), 1)
    lane_sum = ((pc & (SUB - 1)) == (pr >> 7)).astype(BF16)
    scores = jnp.where((rxc < SUB) & lower, _dot(acat, lane_sum), 0.0)
    for m in levels:
        qk = jnp.concatenate([(q if (i // m) % 2 else k)[i:i + m] for i in range(0, r, m)], axis=0)
        ref = _sublane_bcast(cum, m - 1, 2 * m)
        f = (qk * jnp.exp(_neg_abs(cum - ref))).astype(BF16)
        scores = jnp.where((rxc >= m) & (rxc < 2 * m) & lower, _dot_nt(f, f), scores)
    return scores


def _intra_scores_direct(qin, k, cum, mask):
    kout = (k * jnp.exp(-cum)).astype(BF16)
    return jnp.where(mask, _dot_nt(qin, kout), 0.0)


def _out_norm_gate(o, gon, gate):
    return (_rms(o, gon) * gate).astype(BF16)


def _gla_prompt_head(mild, h, rows, q_s, k_s, cum_s, v_s, gate_s, og_ref, s_in, s_out, gon_ref, rxc,
                     lower):
    ks = slice(h * GLA_DK, (h + 1) * GLA_DK)
    vs = slice(h * GLA_DV, (h + 1) * GLA_DV)
    q, k, cum = q_s[rows, ks], k_s[rows, ks], cum_s[rows, ks]
    vb = v_s[rows, vs]
    state = s_in[h]
    qin = (q * jnp.exp(cum)).astype(BF16)
    if mild:
        scores = _intra_scores_direct(qin, k, cum, lower)
    else:
        scores = _intra_scores_factored(q, k, cum, (64, 32, 16, 8), rxc, lower)
    o = _dot(jnp.concatenate([scores.astype(BF16), qin], axis=1),
             jnp.concatenate([vb, state.astype(BF16)], axis=0))
    k_t, cum_t = k.T, cum.T
    last = cum_t[:, CHUNK - 1:CHUNK]
    kd_t = (k_t * jnp.exp(last - cum_t)).astype(BF16)
    s_out[h] = jnp.exp(last) * state + _dot(kd_t, vb)
    og_ref[rows, vs] = _out_norm_gate(o, gon_ref[...], gate_s[rows, vs])


def _cmlp_in_steps(x1, g1_ref, wi_ref, lng_ref, lnb_ref):
    w = CMLP_WIDTH
    hb = _rms(x1, g1_ref[...]).astype(BF16)
    yield None
    u = _dot(hb, wi_ref[:, 0:w])
    yield None
    vn = _layernorm(_dot(hb, wi_ref[:, w:2 * w]), lng_ref[...], lnb_ref[...])
    yield None
    ug = u * _silu(_dot(hb, wi_ref[:, 2 * w:3 * w]))
    yield ug, vn


def _cmlp_out_steps(x1, ug, vn, nchunks, mix_block, ws_ref, bs_ref, wo2_ref, gf_ref):
    vnb = vn.astype(BF16)
    rxc, lower = _index_masks(CHUNK)
    mix_mask = (rxc < mix_block) & lower
    zs = []
    for g in range(CMLP_GROUPS):
        gs = slice(g * CMLP_GROUP_DIM, (g + 1) * CMLP_GROUP_DIM)
        wsm = jnp.where(mix_mask, ws_ref[g], 0.0).astype(BF16)
        mixed = jnp.concatenate(
            [_dot(wsm, vnb[c * CHUNK:(c + 1) * CHUNK, gs]) + bs_ref[:, gs] for c in range(nchunks)],
            axis=0)
        zs.append((ug[:, gs] * mixed).astype(BF16))
    yield None
    acc = x1
    for g in range(CMLP_GROUPS):
        gs = slice(g * CMLP_GROUP_DIM, (g + 1) * CMLP_GROUP_DIM)
        acc = acc + _dot(zs[g], wo2_ref[gs, 0:D_MODEL])
        if g < CMLP_GROUPS - 1:
            yield None
    yield _rms(acc, gf_ref[...])


def _drain(gen):
    out = None
    for out in gen:
        pass
    return out


def _prompt_kernel(nchunks, nblk, blk_per_seq, xa_ref, xc_ref, g0_ref, win_ref, wa_ref, ba_ref,
                   gon_ref, wo_ref, g1_ref, wi_ref, lng_ref, lnb_ref, ws_ref, bs_ref, wo2_ref, gf_ref,
                   y_ref, s_ref, q_s, k_s, cum_s, v_s, gate_s, og_s, st_s):
    j = pl.program_id(0)
    valid = j < nblk
    jj = jnp.minimum(j, nblk - 1)
    is_first = lax.rem(jj, blk_per_seq) == 0
    slot = lax.rem(j, 2)
    og_cur, og_prev = og_s.at[slot], og_s.at[1 - slot]

    @pl.when(j == 0)
    def _():
        og_s[...] = jnp.zeros_like(og_s)

    @pl.when(is_first)
    def _():
        s_ref[...] = jnp.zeros_like(s_ref)

    x1 = xc_ref[0] + _dot(og_prev[...], wo_ref[:, 0:D_MODEL])
    hb = _rms(xa_ref[0], g0_ref[...]).astype(BF16)
    la = _log_decay(_dot(hb, win_ref[:, _A0:GLA_IN_PAD]), wa_ref, ba_ref)
    q_s[...] = _dot(hb, win_ref[:, _Q0:_K0]) * (GLA_DK ** -0.5)
    k_s[...] = _dot(hb, win_ref[:, _K0:_V0])
    rxc, lower = _index_masks(CHUNK)
    tri = lower.astype(BF16)
    total = None
    for c in range(nchunks):
        rows = slice(c * CHUNK, (c + 1) * CHUNK)
        cum = _masked_prefix_sum(tri, la[rows])
        cum_s[rows, :] = cum
        end = jnp.min(cum[CHUNK - SUB:CHUNK, :])
        total = end if total is None else jnp.minimum(total, end)
    mild = total > -DECAY_GUARD
    cmlp_in = _cmlp_in_steps(x1, g1_ref, wi_ref, lng_ref, lnb_ref)
    next(cmlp_in)
    v_s[...] = _dot(hb, win_ref[:, _V0:_G0]).astype(BF16)
    gate_s[...] = _silu(_dot(hb, win_ref[:, _G0:_A0]))

    def head(c, h):
        _gla_prompt_head(True, h, slice(c * CHUNK, (c + 1) * CHUNK), q_s, k_s, cum_s, v_s, gate_s,
                         og_cur, s_ref.at[0] if c == 0 else st_s, st_s, gon_ref, rxc, lower)

    heads = [(c, h) for c in range(nchunks) for h in range(GLA_HEADS)]
    ug_vn = None
    for c, h in heads[:GLA_HEADS]:
        head(c, h)
        if ug_vn is None:
            ug_vn = next(cmlp_in)
    ug_vn = ug_vn if ug_vn is not None else _drain(cmlp_in)
    cmlp_out = _cmlp_out_steps(x1, *ug_vn, nchunks, CHUNK, ws_ref, bs_ref, wo2_ref, gf_ref)
    y = next(cmlp_out)
    for c, h in heads[GLA_HEADS:]:
        head(c, h)
        if y is None:
            y = next(cmlp_out)
    y_ref[0] = y if y is not None else _drain(cmlp_out)
    commit = jnp.logical_and(valid, mild)
    for h in range(GLA_HEADS):
        s_ref[0, h] = jnp.where(commit, st_s[h], s_ref[0, h])

    @pl.when(jnp.logical_and(valid, jnp.logical_not(mild)))
    def _():
        for c in range(nchunks):
            for h in range(GLA_HEADS):
                _gla_prompt_head(False, h, slice(c * CHUNK, (c + 1) * CHUNK), q_s, k_s, cum_s, v_s,
                                 gate_s, og_cur, s_ref.at[0], s_ref.at[0], gon_ref, rxc, lower)


def _const_spec(shape):
    nd = len(shape)
    return pl.BlockSpec(shape, lambda *_: (0,) * nd, pipeline_mode=pl.Buffered(1))


def _prompt_trunk(x, p, ws_eff, bs_full, nchunks):
    b, t, d = x.shape
    rb = nchunks * CHUNK
    bps = t // rb
    nblk = b * bps
    consts = [p["g0"], p["w_in"], p["wa"], p["ba"], p["gon"], p["wo"], p["g1"], p["wi"], p["lng"],
              p["lnb"], ws_eff, bs_full, p["wo2"], p["gf"]]

    def cur(j):
        jj = jnp.minimum(j, nblk - 1)
        return jj // bps, jj % bps

    def prev(j):
        jj = jnp.maximum(j - 1, 0)
        return jj // bps, jj % bps

    return pl.pallas_call(
        functools.partial(_prompt_kernel, nchunks, nblk, bps),
        grid=(nblk + 1,),
        in_specs=[pl.BlockSpec((1, rb, d), lambda j: (*cur(j), 0)),
                  pl.BlockSpec((1, rb, d), lambda j: (*prev(j), 0))]
        + [_const_spec(c.shape) for c in consts],
        out_specs=[pl.BlockSpec((1, rb, d), lambda j: (*prev(j), 0)),
                   pl.BlockSpec((1, GLA_HEADS, GLA_DK, GLA_DV), lambda j: (cur(j)[0], 0, 0, 0))],
        out_shape=[jax.ShapeDtypeStruct((b, t, d), F32),
                   jax.ShapeDtypeStruct((b, GLA_HEADS, GLA_DK, GLA_DV), F32)],
        scratch_shapes=[pltpu.VMEM((rb, GLA_KW), F32), pltpu.VMEM((rb, GLA_KW), F32),
                        pltpu.VMEM((rb, GLA_KW), F32), pltpu.VMEM((rb, GLA_VW), BF16),
                        pltpu.VMEM((rb, GLA_VW), F32), pltpu.VMEM((2, rb, GLA_VW), BF16),
                        pltpu.VMEM((GLA_HEADS, GLA_DK, GLA_DV), F32)],
        compiler_params=pltpu.CompilerParams(dimension_semantics=("arbitrary",),
                                             vmem_limit_bytes=VMEM_LIMIT),
        name="prompt_trunk",
    )(x, x, *consts)


def _gla_in_kernel(x_ref, g_ref, w_ref, wa_ref, ba_ref, q_ref, k_ref, v_ref, gate_ref, la_ref):
    hb = _rms(x_ref[...], g_ref[...]).astype(BF16)
    q_ref[...] = _dot(hb, w_ref[:, _Q0:_K0]) * (GLA_DK ** -0.5)
    k_ref[...] = _dot(hb, w_ref[:, _K0:_V0])
    v_ref[...] = _dot(hb, w_ref[:, _V0:_G0]).astype(BF16)
    gate_ref[...] = _silu(_dot(hb, w_ref[:, _G0:_A0]))
    la_ref[...] = _log_decay(_dot(hb, w_ref[:, _A0:GLA_IN_PAD]), wa_ref, ba_ref)


def _gla_in(x2d, p, tm):
    n = x2d.shape[0]
    row = lambda w: pl.BlockSpec((tm, w), lambda i: (i, 0))
    consts = [p["g0"], p["w_in"], p["wa"], p["ba"]]
    return pl.pallas_call(
        _gla_in_kernel,
        grid=(n // tm,),
        in_specs=[row(D_MODEL)] + [_const_spec(c.shape) for c in consts],
        out_specs=[row(GLA_KW), row(GLA_KW), row(GLA_VW), row(GLA_VW), row(GLA_KW)],
        out_shape=[jax.ShapeDtypeStruct((n, GLA_KW), F32), jax.ShapeDtypeStruct((n, GLA_KW), F32),
                   jax.ShapeDtypeStruct((n, GLA_VW), BF16), jax.ShapeDtypeStruct((n, GLA_VW), F32),
                   jax.ShapeDtypeStruct((n, GLA_KW), F32)],
        compiler_params=pltpu.CompilerParams(dimension_semantics=("arbitrary",),
                                             vmem_limit_bytes=VMEM_LIMIT),
        name="gla_in",
    )(x2d, *consts)


def _gla_sample_heads(mild, nseq, tlen, q_ref, k_ref, v_ref, gate_ref, cum_s, gon_ref, s_in_ref,
                      og_ref, s_out_ref, rxc, lower):
    r = nseq * tlen
    block_mask = (rxc < tlen) & lower
    seq_of_lane = lax.broadcasted_iota(jnp.int32, (GLA_DK, r), 1) >> 3
    for h in range(GLA_HEADS):
        ks = slice(h * GLA_DK, (h + 1) * GLA_DK)
        vs = slice(h * GLA_DV, (h + 1) * GLA_DV)
        q, k, cum = q_ref[:, ks], k_ref[:, ks], cum_s[:, ks]
        vb = v_ref[:, vs]
        qin = (q * jnp.exp(cum)).astype(BF16)
        if mild:
            scores = _intra_scores_direct(qin, k, cum, block_mask)
        else:
            scores = _intra_scores_factored(q, k, cum, (), rxc, lower)
        o_intra = _dot(scores.astype(BF16), vb)
        kd_t = (k * jnp.exp(_sublane_bcast(cum, tlen - 1, tlen) - cum)).T
        cum_t = cum.T
        for i in range(nseq):
            rs = slice(i * tlen, (i + 1) * tlen)
            state = s_in_ref[i, h]
            o = o_intra[rs] + _dot(qin[rs], state.astype(BF16))
            og_ref[rs, vs] = _out_norm_gate(o, gon_ref[...], gate_ref[rs, vs])
            last = cum_t[:, (i + 1) * tlen - 1:(i + 1) * tlen]
            kd_i = jnp.where(seq_of_lane == i, kd_t, 0.0).astype(BF16)
            s_out_ref[i, h] = jnp.exp(last) * state + _dot(kd_i, vb)


def _gla_sample_kernel(nseq, tlen, q_ref, k_ref, v_ref, gate_ref, la_ref, gon_ref, s_in_ref,
                       og_ref, s_out_ref, cum_s):
    rxc, lower = _index_masks(nseq * tlen)
    cum = _masked_prefix_sum(((rxc < tlen) & lower).astype(BF16), la_ref[...])
    cum_s[...] = cum
    mild = jnp.min(cum) > -DECAY_GUARD
    args = (nseq, tlen, q_ref, k_ref, v_ref, gate_ref, cum_s, gon_ref, s_in_ref, og_ref, s_out_ref,
            rxc, lower)
    pl.when(mild)(functools.partial(_gla_sample_heads, True, *args))
    pl.when(jnp.logical_not(mild))(functools.partial(_gla_sample_heads, False, *args))


def _gla_sample(q, k, v, gate, la, gon, state, nseq, tlen):
    n = q.shape[0]
    r = nseq * tlen
    assert tlen == SUB
    row = lambda w: pl.BlockSpec((r, w), lambda i: (i, 0))
    st = pl.BlockSpec((nseq, GLA_HEADS, GLA_DK, GLA_DV), lambda i: (i, 0, 0, 0))
    return pl.pallas_call(
        functools.partial(_gla_sample_kernel, nseq, tlen),
        grid=(n // r,),
        in_specs=[row(GLA_KW), row(GLA_KW), row(GLA_VW), row(GLA_VW), row(GLA_KW),
                  _const_spec((1, GLA_DV)), st],
        out_specs=[row(GLA_VW), st],
        out_shape=[jax.ShapeDtypeStruct((n, GLA_VW), BF16),
                   jax.ShapeDtypeStruct(state.shape, F32)],
        scratch_shapes=[pltpu.VMEM((r, GLA_KW), F32)],
        compiler_params=pltpu.CompilerParams(dimension_semantics=("arbitrary",),
                                             vmem_limit_bytes=VMEM_LIMIT),
        name="gla_sample",
    )(q, k, v, gate, la, gon, state)


def _out_cmlp_kernel(mix_block, x_ref, og_ref, wo_ref, g1_ref, wi_ref, lng_ref, lnb_ref, ws_ref,
                     bs_ref, wo2_ref, gf_ref, y_ref, v_out_ref):
    x1 = x_ref[...] + _dot(og_ref[...], wo_ref[:, 0:D_MODEL])
    ug, vn = _drain(_cmlp_in_steps(x1, g1_ref, wi_ref, lng_ref, lnb_ref))
    v_out_ref[...] = vn
    y_ref[...] = _drain(_cmlp_out_steps(x1, ug, vn, 1, mix_block, ws_ref, bs_ref, wo2_ref, gf_ref))


def _out_cmlp(x2d, og, p, ws_eff, bs_full, mix_block):
    n = x2d.shape[0]
    row = lambda w: pl.BlockSpec((CHUNK, w), lambda i: (i, 0))
    consts = [p["wo"], p["g1"], p["wi"], p["lng"], p["lnb"], ws_eff, bs_full, p["wo2"], p["gf"]]
    return pl.pallas_call(
        functools.partial(_out_cmlp_kernel, mix_block),
        grid=(n // CHUNK,),
        in_specs=[row(D_MODEL), row(GLA_VW)] + [_const_spec(c.shape) for c in consts],
        out_specs=[row(D_MODEL), row(CMLP_WIDTH)],
        out_shape=[jax.ShapeDtypeStruct((n, D_MODEL), F32),
                   jax.ShapeDtypeStruct((n, CMLP_WIDTH), F32)],
        compiler_params=pltpu.CompilerParams(dimension_semantics=("arbitrary",),
                                             vmem_limit_bytes=VMEM_LIMIT),
        name="out_cmlp",
    )(x2d, og, *consts)


def kernel(x_prompt, x_sample, state_gla, norm_g, gla_w_in, gla_w_a_up, gla_b_a_up, gla_g_onorm,
           gla_w_out, cmlp_w_in, cmlp_ln_g, cmlp_ln_b, cmlp_w_spatial, cmlp_b_spatial, cmlp_w_out,
           norm_final):
    b, t, d = x_prompt.shape
    nb, nt, _ = x_sample.shape
    assert t % (2 * CHUNK) == 0 and (nb * nt) % CHUNK == 0 and CHUNK % nt == 0

    p = {
        "g0": norm_g[0].reshape(1, d),
        "g1": norm_g[1].reshape(1, d),
        "w_in": jnp.pad(gla_w_in[0], ((0, 0), (0, GLA_IN_PAD - gla_w_in.shape[2]))).astype(BF16),
        "wa": jnp.pad(gla_w_a_up[0], ((0, LANE - GLA_GATE_RANK), (0, 0))).astype(BF16),
        "ba": gla_b_a_up[0].reshape(1, GLA_KW),
        "gon": gla_g_onorm[0].reshape(1, GLA_DV),
        "wo": _pad_cols_odd_tiles(gla_w_out[0]),
        "wi": _pad_cols_odd_tiles(cmlp_w_in[0]),
        "lng": cmlp_ln_g[0].reshape(1, CMLP_WIDTH),
        "lnb": cmlp_ln_b[0].reshape(1, CMLP_WIDTH),
        "wo2": _pad_cols_odd_tiles(cmlp_w_out[0]),
        "gf": norm_final.reshape(1, d),
    }
    assert p["w_in"].shape[1] // LANE % 2 == 1
    ws = cmlp_w_spatial[0]
    bs = cmlp_b_spatial[0]

    def spatial_params(c):
        reps = CHUNK // c
        ws_eff = jnp.tile(ws[:, :c, :c], (1, reps, reps))
        bs_full = jnp.repeat(jnp.tile(bs[:, :c], (1, reps)).T, CMLP_GROUP_DIM, axis=1)
        return ws_eff, bs_full

    y_prompt, s_prompt = _prompt_trunk(x_prompt, p, *spatial_params(CHUNK), 2)

    xs = x_sample.reshape(nb * nt, d)
    q, k, v, gate, la = _gla_in(xs, p, 512)
    og, s_sample = _gla_sample(q, k, v, gate, la, p["gon"], state_gla[0], 8, nt)
    y_sample, v_rows = _out_cmlp(xs, og, p, *spatial_params(nt), nt)

    return (y_prompt, y_sample.reshape(nb, nt, d), s_prompt[None], s_sample[None],
            v_rows.reshape(1, nb, nt, CMLP_WIDTH))
```

```python
import functools

import jax
import jax.numpy as jnp
from jax import lax
from jax.experimental import pallas as pl
from jax.experimental.pallas import tpu as pltpu

F32 = jnp.float32
BF16 = jnp.bfloat16

D_MODEL = 1024
GLA_HEADS = 4
GLA_DK = 128
GLA_DV = 256
GLA_KW = GLA_HEADS * GLA_DK
GLA_VW = GLA_HEADS * GLA_DV
GLA_GATE_RANK = 16
GLA_GATE_TAU = 16.0
CMLP_WIDTH = 1024
CMLP_GROUPS = 4
CMLP_GROUP_DIM = CMLP_WIDTH // CMLP_GROUPS
LANE = 128
CHUNK = 128
SUB = 8
PROMPT_CHUNKS_PER_STEP = 4
EPS = 1e-6
VMEM_LIMIT = 56 * 1024 * 1024
DECAY_GUARD = 60.0

_Q0, _K0, _V0, _G0, _A0 = 0, GLA_KW, 2 * GLA_KW, 2 * GLA_KW + GLA_VW, 2 * GLA_KW + 2 * GLA_VW
GLA_IN_PAD = _A0 + LANE


def _pad_cols_odd_tiles(w):
    tiles = -(-w.shape[1] // LANE)
    tiles += 1 - tiles % 2
    return jnp.pad(w, ((0, 0), (0, tiles * LANE - w.shape[1]))).astype(BF16)


def _dot(a, b):
    return jnp.dot(a, b, preferred_element_type=F32)


def _dot_nt(a, b):
    return lax.dot_general(a, b, (((1,), (1,)), ((), ())), preferred_element_type=F32)


def _rms(x, g):
    return x * lax.rsqrt(jnp.mean(x * x, axis=-1, keepdims=True) + EPS) * g


def _silu(x):
    return x * jax.nn.sigmoid(x)


def _neg_abs(x):
    return lax.bitcast_convert_type(
        lax.bitcast_convert_type(x, jnp.uint32) | jnp.uint32(0x80000000), F32)


def _split3(x):
    hi = x.astype(BF16)
    r1 = x - hi.astype(F32)
    mid = r1.astype(BF16)
    lo = (r1 - mid.astype(F32)).astype(BF16)
    return hi, mid, lo


def _masked_prefix_sum(mask01, x):
    hi, mid, lo = _split3(x)
    return (_dot(jnp.concatenate([mask01, mask01], axis=1), jnp.concatenate([hi, mid], axis=0))
            + _dot(mask01, lo))


def _sublane_bcast(x, row, group):
    r, n = x.shape
    x3 = x.reshape(r // group, group, n)
    return jnp.broadcast_to(x3[:, row:row + 1, :], x3.shape).reshape(r, n)


def _index_masks(r):
    row = lax.broadcasted_iota(jnp.int32, (r, r), 0)
    col = lax.broadcasted_iota(jnp.int32, (r, r), 1)
    return row ^ col, col <= row


def _log_decay(a_low, wa_ref, ba_ref):
    z = _dot(a_low.astype(BF16), wa_ref[...]) + ba_ref[...]
    log_sig = jnp.minimum(z, 0.0) - jnp.log1p(jnp.exp(-jnp.abs(z)))
    return log_sig * (1.0 / GLA_GATE_TAU)


def _layernorm(v, g, b):
    vc = v - jnp.mean(v, axis=-1, keepdims=True)
    var = jnp.mean(vc * vc, axis=-1, keepdims=True)
    return vc * lax.rsqrt(var + EPS) * g + b


def _intra_scores_factored(q, k, cum, levels, rxc, lower):
    r = q.shape[0]
    parts = []
    for s in range(SUB):
        kb = _sublane_bcast(k, s, SUB)
        cb = _sublane_bcast(cum, s, SUB)
        parts.append((q * kb * jnp.exp(jnp.minimum(cum - cb, 0.0))).astype(BF16))
    acat = jnp.concatenate(parts, axis=1)
    pr = lax.broadcasted_iota(jnp.int32, (SUB * GLA_DK, r), 0)
    pc = lax.broadcasted_iota(jnp.int32, (SUB * GLA_DK, r), 1)
    lane_sum = ((pc & (SUB - 1)) == (pr >> 7)).astype(BF16)
    scores = jnp.where((rxc < SUB) & lower, _dot(acat, lane_sum), 0.0)
    for m in levels:
        qk = jnp.concatenate([(q if (i // m) % 2 else k)[i:i + m] for i in range(0, r, m)], axis=0)
        ref = _sublane_bcast(cum, m - 1, 2 * m)
        f = (qk * jnp.exp(_neg_abs(cum - ref))).astype(BF16)
        scores = jnp.where((rxc >= m) & (rxc < 2 * m) & lower, _dot_nt(f, f), scores)
    return scores


def _intra_scores_direct(qin, k, cum, mask):
    kout = (k * jnp.exp(-cum)).astype(BF16)
    return jnp.where(mask, _dot_nt(qin, kout), 0.0)


def _out_norm_gate(o, gon, gate):
    return (_rms(o, gon) * gate).astype(BF16)


def _gla_prompt_head(mild, h, rows, q_s, k_s, cum_s, v_s, gate_s, og_ref, s_in, s_out, gon_ref, rxc,
                     lower):
    ks = slice(h * GLA_DK, (h + 1) * GLA_DK)
    vs = slice(h * GLA_DV, (h + 1) * GLA_DV)
    q, k, cum = q_s[rows, ks], k_s[rows, ks], cum_s[rows, ks]
    vb = v_s[rows, vs]
    state = s_in[h]
    qin = (q * jnp.exp(cum)).astype(BF16)
    if mild:
        scores = _intra_scores_direct(qin, k, cum, lower)
    else:
        scores = _intra_scores_factored(q, k, cum, (64, 32, 16, 8), rxc, lower)
    o = _dot(jnp.concatenate([scores.astype(BF16), qin], axis=1),
             jnp.concatenate([vb, state.astype(BF16)], axis=0))
    k_t, cum_t = k.T, cum.T
    last = cum_t[:, CHUNK - 1:CHUNK]
    kd_t = (k_t * jnp.exp(last - cum_t)).astype(BF16)
    s_out[h] = jnp.exp(last) * state + _dot(kd_t, vb)
    og_ref[rows, vs] = _out_norm_gate(o, gon_ref[...], gate_s[rows, vs])


def _cmlp_in_steps(x1, g1_ref, wi_ref, lng_ref, lnb_ref):
    w = CMLP_WIDTH
    hb = _rms(x1, g1_ref[...]).astype(BF16)
    yield None
    u = _dot(hb, wi_ref[:, 0:w])
    yield None
    vn = _layernorm(_dot(hb, wi_ref[:, w:2 * w]), lng_ref[...], lnb_ref[...])
    yield None
    ug = u * _silu(_dot(hb, wi_ref[:, 2 * w:3 * w]))
    yield ug, vn


def _cmlp_out_steps(x1, ug, vn, nchunks, mix_block, ws_ref, bs_ref, wo2_ref, gf_ref):
    vnb = vn.astype(BF16)
    rxc, lower = _index_masks(CHUNK)
    mix_mask = (rxc < mix_block) & lower
    zs = []
    for g in range(CMLP_GROUPS):
        gs = slice(g * CMLP_GROUP_DIM, (g + 1) * CMLP_GROUP_DIM)
        wsm = jnp.where(mix_mask, ws_ref[g], 0.0).astype(BF16)
        mixed = jnp.concatenate(
            [_dot(wsm, vnb[c * CHUNK:(c + 1) * CHUNK, gs]) + bs_ref[:, gs] for c in range(nchunks)],
            axis=0)
        zs.append((ug[:, gs] * mixed).astype(BF16))
    yield None
    acc = x1
    for g in range(CMLP_GROUPS):
        gs = slice(g * CMLP_GROUP_DIM, (g + 1) * CMLP_GROUP_DIM)
        acc = acc + _dot(zs[g], wo2_ref[gs, 0:D_MODEL])
        if g < CMLP_GROUPS - 1:
            yield None
    yield _rms(acc, gf_ref[...])


def _drain(gen):
    out = None
    for out in gen:
        pass
    return out


def _prompt_kernel(nchunks, nblk, blk_per_seq, xa_ref, xc_ref, g0_ref, win_ref, wa_ref, ba_ref,
                   gon_ref, wo_ref, g1_ref, wi_ref, lng_ref, lnb_ref, ws_ref, bs_ref, wo2_ref, gf_ref,
                   y_ref, s_ref, q_s, k_s, cum_s, v_s, gate_s, og_s, st_s):
    j = pl.program_id(0)
    valid = j < nblk
    jj = jnp.minimum(j, nblk - 1)
    is_first = lax.rem(jj, blk_per_seq) == 0
    slot = lax.rem(j, 2)
    og_cur, og_prev = og_s.at[slot], og_s.at[1 - slot]

    @pl.when(j == 0)
    def _():
        og_s[...] = jnp.zeros_like(og_s)

    @pl.when(is_first)
    def _():
        s_ref[...] = jnp.zeros_like(s_ref)

    x1 = xc_ref[0] + _dot(og_prev[...], wo_ref[:, 0:D_MODEL])
    hb = _rms(xa_ref[0], g0_ref[...]).astype(BF16)
    la = _log_decay(_dot(hb, win_ref[:, _A0:GLA_IN_PAD]), wa_ref, ba_ref)
    q_s[...] = _dot(hb, win_ref[:, _Q0:_K0]) * (GLA_DK ** -0.5)
    k_s[...] = _dot(hb, win_ref[:, _K0:_V0])
    rxc, lower = _index_masks(CHUNK)
    tri = lower.astype(BF16)
    total = None
    for c in range(nchunks):
        rows = slice(c * CHUNK, (c + 1) * CHUNK)
        cum = _masked_prefix_sum(tri, la[rows])
        cum_s[rows, :] = cum
        end = jnp.min(cum[CHUNK - SUB:CHUNK, :])
        total = end if total is None else jnp.minimum(total, end)
    mild = total > -DECAY_GUARD
    cmlp_in = _cmlp_in_steps(x1, g1_ref, wi_ref, lng_ref, lnb_ref)
    next(cmlp_in)
    v_s[...] = _dot(hb, win_ref[:, _V0:_G0]).astype(BF16)
    gate_s[...] = _silu(_dot(hb, win_ref[:, _G0:_A0]))

    def head(c, h):
        _gla_prompt_head(True, h, slice(c * CHUNK, (c + 1) * CHUNK), q_s, k_s, cum_s, v_s, gate_s,
                         og_cur, s_ref.at[0] if c == 0 else st_s, st_s, gon_ref, rxc, lower)

    heads = [(c, h) for c in range(nchunks) for h in range(GLA_HEADS)]
    ug_vn = None
    for c, h in heads[:GLA_HEADS]:
        head(c, h)
        if ug_vn is None:
            ug_vn = next(cmlp_in)
    ug_vn = ug_vn if ug_vn is not None else _drain(cmlp_in)
    cmlp_out = _cmlp_out_steps(x1, *ug_vn, nchunks, CHUNK, ws_ref, bs_ref, wo2_ref, gf_ref)
    y = next(cmlp_out)
    for c, h in heads[GLA_HEADS:]:
        head(c, h)
        if y is None:
            y = next(cmlp_out)
    y_ref[0] = y if y is not None else _drain(cmlp_out)
    commit = jnp.logical_and(valid, mild)
    for h in range(GLA_HEADS):
        s_ref[0, h] = jnp.where(commit, st_s[h], s_ref[0, h])

    @pl.when(jnp.logical_and(valid, jnp.logical_not(mild)))
    def _():
        for c in range(nchunks):
            for h in range(GLA_HEADS):
                _gla_prompt_head(False, h, slice(c * CHUNK, (c + 1) * CHUNK), q_s, k_s, cum_s, v_s,
                                 gate_s, og_cur, s_ref.at[0], s_ref.at[0], gon_ref, rxc, lower)


def _const_spec(shape):
    nd = len(shape)
    return pl.BlockSpec(shape, lambda *_: (0,) * nd, pipeline_mode=pl.Buffered(1))


def _prompt_trunk(x, p, ws_eff, bs_full, nchunks):
    b, t, d = x.shape
    rb = nchunks * CHUNK
    bps = t // rb
    nblk = b * bps
    consts = [p["g0"], p["w_in"], p["wa"], p["ba"], p["gon"], p["wo"], p["g1"], p["wi"], p["lng"],
              p["lnb"], ws_eff, bs_full, p["wo2"], p["gf"]]

    def cur(j):
        jj = jnp.minimum(j, nblk - 1)
        return jj // bps, jj % bps

    def prev(j):
        jj = jnp.maximum(j - 1, 0)
        return jj // bps, jj % bps

    return pl.pallas_call(
        functools.partial(_prompt_kernel, nchunks, nblk, bps),
        grid=(nblk + 1,),
        in_specs=[pl.BlockSpec((1, rb, d), lambda j: (*cur(j), 0)),
                  pl.BlockSpec((1, rb, d), lambda j: (*prev(j), 0))]
        + [_const_spec(c.shape) for c in consts],
        out_specs=[pl.BlockSpec((1, rb, d), lambda j: (*prev(j), 0)),
                   pl.BlockSpec((1, GLA_HEADS, GLA_DK, GLA_DV), lambda j: (cur(j)[0], 0, 0, 0))],
        out_shape=[jax.ShapeDtypeStruct((b, t, d), F32),
                   jax.ShapeDtypeStruct((b, GLA_HEADS, GLA_DK, GLA_DV), F32)],
        scratch_shapes=[pltpu.VMEM((rb, GLA_KW), F32), pltpu.VMEM((rb, GLA_KW), F32),
                        pltpu.VMEM((rb, GLA_KW), F32), pltpu.VMEM((rb, GLA_VW), BF16),
                        pltpu.VMEM((rb, GLA_VW), F32), pltpu.VMEM((2, rb, GLA_VW), BF16),
                        pltpu.VMEM((GLA_HEADS, GLA_DK, GLA_DV), F32)],
        compiler_params=pltpu.CompilerParams(dimension_semantics=("arbitrary",),
                                             vmem_limit_bytes=VMEM_LIMIT),
        name="prompt_trunk",
    )(x, x, *consts)


def _gla_in_kernel(x_ref, g_ref, w_ref, wa_ref, ba_ref, q_ref, k_ref, v_ref, gate_ref, la_ref):
    hb = _rms(x_ref[...], g_ref[...]).astype(BF16)
    q_ref[...] = _dot(hb, w_ref[:, _Q0:_K0]) * (GLA_DK ** -0.5)
    k_ref[...] = _dot(hb, w_ref[:, _K0:_V0])
    v_ref[...] = _dot(hb, w_ref[:, _V0:_G0]).astype(BF16)
    gate_ref[...] = _silu(_dot(hb, w_ref[:, _G0:_A0]))
    la_ref[...] = _log_decay(_dot(hb, w_ref[:, _A0:GLA_IN_PAD]), wa_ref, ba_ref)


def _gla_in(x2d, p, tm):
    n = x2d.shape[0]
    row = lambda w: pl.BlockSpec((tm, w), lambda i: (i, 0))
    consts = [p["g0"], p["w_in"], p["wa"], p["ba"]]
    return pl.pallas_call(
        _gla_in_kernel,
        grid=(n // tm,),
        in_specs=[row(D_MODEL)] + [_const_spec(c.shape) for c in consts],
        out_specs=[row(GLA_KW), row(GLA_KW), row(GLA_VW), row(GLA_VW), row(GLA_KW)],
        out_shape=[jax.ShapeDtypeStruct((n, GLA_KW), F32), jax.ShapeDtypeStruct((n, GLA_KW), F32),
                   jax.ShapeDtypeStruct((n, GLA_VW), BF16), jax.ShapeDtypeStruct((n, GLA_VW), F32),
                   jax.ShapeDtypeStruct((n, GLA_KW), F32)],
        compiler_params=pltpu.CompilerParams(dimension_semantics=("arbitrary",),
                                             vmem_limit_bytes=VMEM_LIMIT),
        name="gla_in",
    )(x2d, *consts)


def _gla_sample_heads(mild, nseq, tlen, q_ref, k_ref, v_ref, gate_ref, cum_s, gon_ref, s_in_ref,
                      og_ref, s_out_ref, rxc, lower):
    r = nseq * tlen
    block_mask = (rxc < tlen) & lower
    seq_of_lane = lax.broadcasted_iota(jnp.int32, (GLA_DK, r), 1) >> 3
    for h in range(GLA_HEADS):
        ks = slice(h * GLA_DK, (h + 1) * GLA_DK)
        vs = slice(h * GLA_DV, (h + 1) * GLA_DV)
        q, k, cum = q_ref[:, ks], k_ref[:, ks], cum_s[:, ks]
        vb = v_ref[:, vs]
        qin = (q * jnp.exp(cum)).astype(BF16)
        if mild:
            scores = _intra_scores_direct(qin, k, cum, block_mask)
        else:
            scores = _intra_scores_factored(q, k, cum, (), rxc, lower)
        o_intra = _dot(scores.astype(BF16), vb)
        kd_t = (k * jnp.exp(_sublane_bcast(cum, tlen - 1, tlen) - cum)).T
        cum_t = cum.T
        for i in range(nseq):
            rs = slice(i * tlen, (i + 1) * tlen)
            state = s_in_ref[i, h]
            o = o_intra[rs] + _dot(qin[rs], state.astype(BF16))
            og_ref[rs, vs] = _out_norm_gate(o, gon_ref[...], gate_ref[rs, vs])
            last = cum_t[:, (i + 1) * tlen - 1:(i + 1) * tlen]
            kd_i = jnp.where(seq_of_lane == i, kd_t, 0.0).astype(BF16)
            s_out_ref[i, h] = jnp.exp(last) * state + _dot(kd_i, vb)


def _gla_sample_kernel(nseq, tlen, q_ref, k_ref, v_ref, gate_ref, la_ref, gon_ref, s_in_ref,
                       og_ref, s_out_ref, cum_s):
    rxc, lower = _index_masks(nseq * tlen)
    cum = _masked_prefix_sum(((rxc < tlen) & lower).astype(BF16), la_ref[...])
    cum_s[...] = cum
    mild = jnp.min(cum) > -DECAY_GUARD
    args = (nseq, tlen, q_ref, k_ref, v_ref, gate_ref, cum_s, gon_ref, s_in_ref, og_ref, s_out_ref,
            rxc, lower)
    pl.when(mild)(functools.partial(_gla_sample_heads, True, *args))
    pl.when(jnp.logical_not(mild))(functools.partial(_gla_sample_heads, False, *args))


def _gla_sample(q, k, v, gate, la, gon, state, nseq, tlen):
    n = q.shape[0]
    r = nseq * tlen
    assert tlen == SUB
    row = lambda w: pl.BlockSpec((r, w), lambda i: (i, 0))
    st = pl.BlockSpec((nseq, GLA_HEADS, GLA_DK, GLA_DV), lambda i: (i, 0, 0, 0))
    return pl.pallas_call(
        functools.partial(_gla_sample_kernel, nseq, tlen),
        grid=(n // r,),
        in_specs=[row(GLA_KW), row(GLA_KW), row(GLA_VW), row(GLA_VW), row(GLA_KW),
                  _const_spec((1, GLA_DV)), st],
        out_specs=[row(GLA_VW), st],
        out_shape=[jax.ShapeDtypeStruct((n, GLA_VW), BF16),
                   jax.ShapeDtypeStruct(state.shape, F32)],
        scratch_shapes=[pltpu.VMEM((r, GLA_KW), F32)],
        compiler_params=pltpu.CompilerParams(dimension_semantics=("arbitrary",),
                                             vmem_limit_bytes=VMEM_LIMIT),
        name="gla_sample",
    )(q, k, v, gate, la, gon, state)


def _out_cmlp_kernel(mix_block, x_ref, og_ref, wo_ref, g1_ref, wi_ref, lng_ref, lnb_ref, ws_ref,
                     bs_ref, wo2_ref, gf_ref, y_ref, v_out_ref):
    x1 = x_ref[...] + _dot(og_ref[...], wo_ref[:, 0:D_MODEL])
    ug, vn = _drain(_cmlp_in_steps(x1, g1_ref, wi_ref, lng_ref, lnb_ref))
    v_out_ref[...] = vn
    y_ref[...] = _drain(_cmlp_out_steps(x1, ug, vn, 1, mix_block, ws_ref, bs_ref, wo2_ref, gf_ref))


def _out_cmlp(x2d, og, p, ws_eff, bs_full, mix_block):
    n = x2d.shape[0]
    row = lambda w: pl.BlockSpec((CHUNK, w), lambda i: (i, 0))
    consts = [p["wo"], p["g1"], p["wi"], p["lng"], p["lnb"], ws_eff, bs_full, p["wo2"], p["gf"]]
    return pl.pallas_call(
        functools.partial(_out_cmlp_kernel, mix_block),
        grid=(n // CHUNK,),
        in_specs=[row(D_MODEL), row(GLA_VW)] + [_const_spec(c.shape) for c in consts],
        out_specs=[row(D_MODEL), row(CMLP_WIDTH)],
        out_shape=[jax.ShapeDtypeStruct((n, D_MODEL), F32),
                   jax.ShapeDtypeStruct((n, CMLP_WIDTH), F32)],
        compiler_params=pltpu.CompilerParams(dimension_semantics=("arbitrary",),
                                             vmem_limit_bytes=VMEM_LIMIT),
        name="out_cmlp",
    )(x2d, og, *consts)


def kernel(x_prompt, x_sample, state_gla, norm_g, gla_w_in, gla_w_a_up, gla_b_a_up, gla_g_onorm,
           gla_w_out, cmlp_w_in, cmlp_ln_g, cmlp_ln_b, cmlp_w_spatial, cmlp_b_spatial, cmlp_w_out,
           norm_final):
    b, t, d = x_prompt.shape
    nb, nt, _ = x_sample.shape
    assert t % (PROMPT_CHUNKS_PER_STEP * CHUNK) == 0 and (nb * nt) % CHUNK == 0 and CHUNK % nt == 0

    p = {
        "g0": norm_g[0].reshape(1, d),
        "g1": norm_g[1].reshape(1, d),
        "w_in": jnp.pad(gla_w_in[0], ((0, 0), (0, GLA_IN_PAD - gla_w_in.shape[2]))).astype(BF16),
        "wa": jnp.pad(gla_w_a_up[0], ((0, LANE - GLA_GATE_RANK), (0, 0))).astype(BF16),
        "ba": gla_b_a_up[0].reshape(1, GLA_KW),
        "gon": gla_g_onorm[0].reshape(1, GLA_DV),
        "wo": _pad_cols_odd_tiles(gla_w_out[0]),
        "wi": _pad_cols_odd_tiles(cmlp_w_in[0]),
        "lng": cmlp_ln_g[0].reshape(1, CMLP_WIDTH),
        "lnb": cmlp_ln_b[0].reshape(1, CMLP_WIDTH),
        "wo2": _pad_cols_odd_tiles(cmlp_w_out[0]),
        "gf": norm_final.reshape(1, d),
    }
    assert p["w_in"].shape[1] // LANE % 2 == 1
    ws = cmlp_w_spatial[0]
    bs = cmlp_b_spatial[0]

    def spatial_params(c):
        reps = CHUNK // c
        ws_eff = jnp.tile(ws[:, :c, :c], (1, reps, reps))
        bs_full = jnp.repeat(jnp.tile(bs[:, :c], (1, reps)).T, CMLP_GROUP_DIM, axis=1)
        return ws_eff, bs_full

    y_prompt, s_prompt = _prompt_trunk(x_prompt, p, *spatial_params(CHUNK), PROMPT_CHUNKS_PER_STEP)

    xs = x_sample.reshape(nb * nt, d)
    q, k, v, gate, la = _gla_in(xs, p, 512)
    og, s_sample = _gla_sample(q, k, v, gate, la, p["gon"], state_gla[0], 8, nt)
    y_sample, v_rows = _out_cmlp(xs, og, p, *spatial_params(nt), nt)

    return (y_prompt, y_sample.reshape(nb, nt, d), s_prompt[None], s_sample[None],
            v_rows.reshape(1, nb, nt, CMLP_WIDTH))
```

```python
import functools

import jax
import jax.numpy as jnp
from jax import lax
from jax.experimental import pallas as pl
from jax.experimental.pallas import tpu as pltpu

F32 = jnp.float32
BF16 = jnp.bfloat16

D_MODEL = 1024
GLA_HEADS = 4
GLA_DK = 128
GLA_DV = 256
GLA_KW = GLA_HEADS * GLA_DK
GLA_VW = GLA_HEADS * GLA_DV
GLA_GATE_RANK = 16
GLA_GATE_TAU = 16.0
CMLP_WIDTH = 1024
CMLP_GROUPS = 4
CMLP_GROUP_DIM = CMLP_WIDTH // CMLP_GROUPS
LANE = 128
CHUNK = 128
SUB = 8
PROMPT_CHUNKS_PER_STEP = 4
SAMPLE_CHUNKS_PER_STEP = 4
SAMPLE_SEQS_PER_STEP = 16
EPS = 1e-6
VMEM_LIMIT = 56 * 1024 * 1024
DECAY_GUARD = 60.0

_Q0, _K0, _V0, _G0, _A0 = 0, GLA_KW, 2 * GLA_KW, 2 * GLA_KW + GLA_VW, 2 * GLA_KW + 2 * GLA_VW
GLA_IN_PAD = _A0 + LANE


def _pad_cols_odd_tiles(w):
    tiles = -(-w.shape[1] // LANE)
    tiles += 1 - tiles % 2
    return _bf16_padded(w, tiles * LANE)


def _bf16_padded(w, cols):
    rows, have = w.shape
    return jnp.concatenate([w.astype(BF16), jnp.zeros((rows, cols - have), BF16)], axis=1)


def _dot(a, b):
    return jnp.dot(a, b, preferred_element_type=F32)


def _dot_nt(a, b):
    return lax.dot_general(a, b, (((1,), (1,)), ((), ())), preferred_element_type=F32)


def _rms(x, g):
    return x * lax.rsqrt(jnp.mean(x * x, axis=-1, keepdims=True) + EPS) * g


def _silu(x):
    return x * jax.nn.sigmoid(x)


def _neg_abs(x):
    return lax.bitcast_convert_type(
        lax.bitcast_convert_type(x, jnp.uint32) | jnp.uint32(0x80000000), F32)


def _split3(x):
    hi = x.astype(BF16)
    r1 = x - hi.astype(F32)
    mid = r1.astype(BF16)
    lo = (r1 - mid.astype(F32)).astype(BF16)
    return hi, mid, lo


def _masked_prefix_sum(mask01, x):
    hi, mid, lo = _split3(x)
    return (_dot(jnp.concatenate([mask01, mask01], axis=1), jnp.concatenate([hi, mid], axis=0))
            + _dot(mask01, lo))


def _sublane_bcast(x, row, group):
    r, n = x.shape
    x3 = x.reshape(r // group, group, n)
    return jnp.broadcast_to(x3[:, row:row + 1, :], x3.shape).reshape(r, n)


def _index_masks(r):
    row = lax.broadcasted_iota(jnp.int32, (r, r), 0)
    col = lax.broadcasted_iota(jnp.int32, (r, r), 1)
    return row ^ col, col <= row


def _log_decay(a_low, wa_ref, ba_ref):
    z = _dot(a_low.astype(BF16), wa_ref[...]) + ba_ref[...]
    log_sig = jnp.minimum(z, 0.0) - jnp.log1p(jnp.exp(-jnp.abs(z)))
    return log_sig * (1.0 / GLA_GATE_TAU)


def _layernorm(v, g, b):
    vc = v - jnp.mean(v, axis=-1, keepdims=True)
    var = jnp.mean(vc * vc, axis=-1, keepdims=True)
    return vc * lax.rsqrt(var + EPS) * g + b


def _intra_scores_factored(q, k, cum, levels, rxc, lower):
    r = q.shape[0]
    parts = []
    for s in range(SUB):
        kb = _sublane_bcast(k, s, SUB)
        cb = _sublane_bcast(cum, s, SUB)
        parts.append((q * kb * jnp.exp(jnp.minimum(cum - cb, 0.0))).astype(BF16))
    acat = jnp.concatenate(parts, axis=1)
    pr = lax.broadcasted_iota(jnp.int32, (SUB * GLA_DK, r), 0)
    pc = lax.broadcasted_iota(jnp.int32, (SUB * GLA_DK, r), 1)
    lane_sum = ((pc & (SUB - 1)) == (pr >> 7)).astype(BF16)
    scores = jnp.where((rxc < SUB) & lower, _dot(acat, lane_sum), 0.0)
    for m in levels:
        qk = jnp.concatenate([(q if (i // m) % 2 else k)[i:i + m] for i in range(0, r, m)], axis=0)
        ref = _sublane_bcast(cum, m - 1, 2 * m)
        f = (qk * jnp.exp(_neg_abs(cum - ref))).astype(BF16)
        scores = jnp.where((rxc >= m) & (rxc < 2 * m) & lower, _dot_nt(f, f), scores)
    return scores


def _intra_scores_direct(qin, k, cum, mask):
    kout = (k * jnp.exp(-cum)).astype(BF16)
    return jnp.where(mask, _dot_nt(qin, kout), 0.0)


def _out_norm_gate(o, gon, gate):
    return (_rms(o, gon) * gate).astype(BF16)


def _gla_prompt_head(mild, h, rows, q_s, k_s, cum_s, v_s, gate_s, og_ref, s_in, s_out, gon_ref, rxc,
                     lower):
    ks = slice(h * GLA_DK, (h + 1) * GLA_DK)
    vs = slice(h * GLA_DV, (h + 1) * GLA_DV)
    q, k, cum = q_s[rows, ks], k_s[rows, ks], cum_s[rows, ks]
    vb = v_s[rows, vs]
    state = s_in[h]
    qin = (q * jnp.exp(cum)).astype(BF16)
    if mild:
        scores = _intra_scores_direct(qin, k, cum, lower)
    else:
        scores = _intra_scores_factored(q, k, cum, (64, 32, 16, 8), rxc, lower)
    o = _dot(jnp.concatenate([scores.astype(BF16), qin], axis=1),
             jnp.concatenate([vb, state.astype(BF16)], axis=0))
    k_t, cum_t = k.T, cum.T
    last = cum_t[:, CHUNK - 1:CHUNK]
    kd_t = (k_t * jnp.exp(last - cum_t)).astype(BF16)
    s_out[h] = jnp.exp(last) * state + _dot(kd_t, vb)
    og_ref[rows, vs] = _out_norm_gate(o, gon_ref[...], gate_s[rows, vs])


def _cmlp_steps(x1, nchunks, mix_block, g1_ref, wi_ref, lng_ref, lnb_ref, ws_ref, bs_ref, wo2_ref,
                gf_ref, v_out_ref=None):
    w = CMLP_WIDTH
    hb = _rms(x1, g1_ref[...]).astype(BF16)
    yield None
    u = _dot(hb, wi_ref[:, 0:w])
    yield None
    vn = _layernorm(_dot(hb, wi_ref[:, w:2 * w]), lng_ref[...], lnb_ref[...])
    if v_out_ref is not None:
        v_out_ref[...] = vn
    yield None
    ug = u * _silu(_dot(hb, wi_ref[:, 2 * w:3 * w]))
    yield None
    vnb = vn.astype(BF16)
    rxc, lower = _index_masks(CHUNK)
    mix_mask = (rxc < mix_block) & lower
    zs = []
    for g in range(CMLP_GROUPS):
        gs = slice(g * CMLP_GROUP_DIM, (g + 1) * CMLP_GROUP_DIM)
        wsm = jnp.where(mix_mask, ws_ref[g], 0.0).astype(BF16)
        mixed = jnp.concatenate(
            [_dot(wsm, vnb[c * CHUNK:(c + 1) * CHUNK, gs]) + bs_ref[:, gs] for c in range(nchunks)],
            axis=0)
        zs.append((ug[:, gs] * mixed).astype(BF16))
    yield None
    acc = x1
    for g in range(CMLP_GROUPS):
        gs = slice(g * CMLP_GROUP_DIM, (g + 1) * CMLP_GROUP_DIM)
        acc = acc + _dot(zs[g], wo2_ref[gs, 0:D_MODEL])
        if g < CMLP_GROUPS - 1:
            yield None
    yield _rms(acc, gf_ref[...])


CMLP_STEPS = 9


def _drain(gen):
    out = None
    for out in gen:
        pass
    return out


def _prompt_kernel(nchunks, nblk, blk_per_seq, xa_ref, xc_ref, g0_ref, win_ref, wa_ref, ba_ref,
                   gon_ref, wo_ref, g1_ref, wi_ref, lng_ref, lnb_ref, ws_ref, bs_ref, wo2_ref, gf_ref,
                   y_ref, s_ref, q_s, k_s, cum_s, v_s, gate_s, og_s, st_s):
    j = pl.program_id(0)
    valid = j < nblk
    jj = jnp.minimum(j, nblk - 1)
    is_first = lax.rem(jj, blk_per_seq) == 0
    slot = lax.rem(j, 2)
    og_cur, og_prev = og_s.at[slot], og_s.at[1 - slot]

    @pl.when(j == 0)
    def _():
        og_s[...] = jnp.zeros_like(og_s)

    @pl.when(is_first)
    def _():
        s_ref[...] = jnp.zeros_like(s_ref)

    x1 = xc_ref[0] + _dot(og_prev[...], wo_ref[:, 0:D_MODEL])
    hb = _rms(xa_ref[0], g0_ref[...]).astype(BF16)
    la = _log_decay(_dot(hb, win_ref[:, _A0:GLA_IN_PAD]), wa_ref, ba_ref)
    q_s[...] = _dot(hb, win_ref[:, _Q0:_K0]) * (GLA_DK ** -0.5)
    k_s[...] = _dot(hb, win_ref[:, _K0:_V0])
    rxc, lower = _index_masks(CHUNK)
    tri = lower.astype(BF16)
    total = None
    for c in range(nchunks):
        rows = slice(c * CHUNK, (c + 1) * CHUNK)
        cum = _masked_prefix_sum(tri, la[rows])
        cum_s[rows, :] = cum
        end = jnp.min(cum[CHUNK - SUB:CHUNK, :])
        total = end if total is None else jnp.minimum(total, end)
    mild = total > -DECAY_GUARD
    cmlp = _cmlp_steps(x1, nchunks, CHUNK, g1_ref, wi_ref, lng_ref, lnb_ref, ws_ref, bs_ref, wo2_ref,
                       gf_ref)
    next(cmlp)
    v_s[...] = _dot(hb, win_ref[:, _V0:_G0]).astype(BF16)
    gate_s[...] = _silu(_dot(hb, win_ref[:, _G0:_A0]))

    heads = [(c, h) for c in range(nchunks) for h in range(GLA_HEADS)]
    per_piece = -(-len(heads) // (CMLP_STEPS - 1))
    y = None
    for i in range(CMLP_STEPS - 1):
        y = next(cmlp)
        for c, h in heads[i * per_piece:(i + 1) * per_piece]:
            _gla_prompt_head(True, h, slice(c * CHUNK, (c + 1) * CHUNK), q_s, k_s, cum_s, v_s, gate_s,
                             og_cur, s_ref.at[0] if c == 0 else st_s, st_s, gon_ref, rxc, lower)
    y_ref[0] = y
    commit = jnp.logical_and(valid, mild)
    for h in range(GLA_HEADS):
        s_ref[0, h] = jnp.where(commit, st_s[h], s_ref[0, h])

    @pl.when(jnp.logical_and(valid, jnp.logical_not(mild)))
    def _():
        for c in range(nchunks):
            for h in range(GLA_HEADS):
                _gla_prompt_head(False, h, slice(c * CHUNK, (c + 1) * CHUNK), q_s, k_s, cum_s, v_s,
                                 gate_s, og_cur, s_ref.at[0], s_ref.at[0], gon_ref, rxc, lower)


def _const_spec(shape):
    nd = len(shape)
    return pl.BlockSpec(shape, lambda *_: (0,) * nd, pipeline_mode=pl.Buffered(1))


def _prompt_trunk(x, p, ws_eff, bs_full, nchunks):
    b, t, d = x.shape
    rb = nchunks * CHUNK
    bps = t // rb
    nblk = b * bps
    consts = [p["g0"], p["w_in"], p["wa"], p["ba"], p["gon"], p["wo"], p["g1"], p["wi"], p["lng"],
              p["lnb"], ws_eff, bs_full, p["wo2"], p["gf"]]

    def cur(j):
        jj = jnp.minimum(j, nblk - 1)
        return jj // bps, jj % bps

    def prev(j):
        jj = jnp.maximum(j - 1, 0)
        return jj // bps, jj % bps

    return pl.pallas_call(
        functools.partial(_prompt_kernel, nchunks, nblk, bps),
        grid=(nblk + 1,),
        in_specs=[pl.BlockSpec((1, rb, d), lambda j: (*cur(j), 0)),
                  pl.BlockSpec((1, rb, d), lambda j: (*prev(j), 0))]
        + [_const_spec(c.shape) for c in consts],
        out_specs=[pl.BlockSpec((1, rb, d), lambda j: (*prev(j), 0)),
                   pl.BlockSpec((1, GLA_HEADS, GLA_DK, GLA_DV), lambda j: (cur(j)[0], 0, 0, 0))],
        out_shape=[jax.ShapeDtypeStruct((b, t, d), F32),
                   jax.ShapeDtypeStruct((b, GLA_HEADS, GLA_DK, GLA_DV), F32)],
        scratch_shapes=[pltpu.VMEM((rb, GLA_KW), F32), pltpu.VMEM((rb, GLA_KW), F32),
                        pltpu.VMEM((rb, GLA_KW), F32), pltpu.VMEM((rb, GLA_VW), BF16),
                        pltpu.VMEM((rb, GLA_VW), F32), pltpu.VMEM((2, rb, GLA_VW), BF16),
                        pltpu.VMEM((GLA_HEADS, GLA_DK, GLA_DV), F32)],
        compiler_params=pltpu.CompilerParams(dimension_semantics=("arbitrary",),
                                             vmem_limit_bytes=VMEM_LIMIT),
        name="prompt_trunk",
    )(x, x, *consts)


def _gla_in_kernel(x_ref, g_ref, w_ref, wa_ref, ba_ref, q_ref, k_ref, v_ref, gate_ref, la_ref):
    hb = _rms(x_ref[...], g_ref[...]).astype(BF16)
    q_ref[...] = _dot(hb, w_ref[:, _Q0:_K0]) * (GLA_DK ** -0.5)
    k_ref[...] = _dot(hb, w_ref[:, _K0:_V0])
    v_ref[...] = _dot(hb, w_ref[:, _V0:_G0]).astype(BF16)
    gate_ref[...] = _silu(_dot(hb, w_ref[:, _G0:_A0]))
    la_ref[...] = _log_decay(_dot(hb, w_ref[:, _A0:GLA_IN_PAD]), wa_ref, ba_ref)


def _gla_in(x2d, p, tm):
    n = x2d.shape[0]
    row = lambda w: pl.BlockSpec((tm, w), lambda i: (i, 0))
    consts = [p["g0"], p["w_in"], p["wa"], p["ba"]]
    return pl.pallas_call(
        _gla_in_kernel,
        grid=(n // tm,),
        in_specs=[row(D_MODEL)] + [_const_spec(c.shape) for c in consts],
        out_specs=[row(GLA_KW), row(GLA_KW), row(GLA_VW), row(GLA_VW), row(GLA_KW)],
        out_shape=[jax.ShapeDtypeStruct((n, GLA_KW), F32), jax.ShapeDtypeStruct((n, GLA_KW), F32),
                   jax.ShapeDtypeStruct((n, GLA_VW), BF16), jax.ShapeDtypeStruct((n, GLA_VW), F32),
                   jax.ShapeDtypeStruct((n, GLA_KW), F32)],
        compiler_params=pltpu.CompilerParams(dimension_semantics=("arbitrary",),
                                             vmem_limit_bytes=VMEM_LIMIT),
        name="gla_in",
    )(x2d, *consts)


def _gla_sample_heads(mild, nseq, tlen, q_ref, k_ref, v_ref, gate_ref, cum_s, gon_ref, s_in_ref,
                      og_ref, s_out_ref, rxc, lower):
    r = nseq * tlen
    block_mask = (rxc < tlen) & lower
    seq_of_lane = lax.broadcasted_iota(jnp.int32, (GLA_DK, r), 1) >> 3
    for h in range(GLA_HEADS):
        ks = slice(h * GLA_DK, (h + 1) * GLA_DK)
        vs = slice(h * GLA_DV, (h + 1) * GLA_DV)
        q, k, cum = q_ref[:, ks], k_ref[:, ks], cum_s[:, ks]
        vb = v_ref[:, vs]
        qin = (q * jnp.exp(cum)).astype(BF16)
        if mild:
            scores = _intra_scores_direct(qin, k, cum, block_mask)
        else:
            scores = _intra_scores_factored(q, k, cum, (), rxc, lower)
        o_intra = _dot(scores.astype(BF16), vb)
        kd_t = (k * jnp.exp(_sublane_bcast(cum, tlen - 1, tlen) - cum)).T
        cum_t = cum.T
        for i in range(nseq):
            rs = slice(i * tlen, (i + 1) * tlen)
            state = s_in_ref[i, h]
            o = o_intra[rs] + _dot(qin[rs], state.astype(BF16))
            og_ref[rs, vs] = _out_norm_gate(o, gon_ref[...], gate_ref[rs, vs])
            last = cum_t[:, (i + 1) * tlen - 1:(i + 1) * tlen]
            kd_i = jnp.where(seq_of_lane == i, kd_t, 0.0).astype(BF16)
            s_out_ref[i, h] = jnp.exp(last) * state + _dot(kd_i, vb)


def _gla_sample_kernel(nseq, tlen, q_ref, k_ref, v_ref, gate_ref, la_ref, gon_ref, s_in_ref,
                       og_ref, s_out_ref, cum_s):
    rxc, lower = _index_masks(nseq * tlen)
    cum = _masked_prefix_sum(((rxc < tlen) & lower).astype(BF16), la_ref[...])
    cum_s[...] = cum
    mild = jnp.min(cum) > -DECAY_GUARD
    args = (nseq, tlen, q_ref, k_ref, v_ref, gate_ref, cum_s, gon_ref, s_in_ref, og_ref, s_out_ref,
            rxc, lower)
    pl.when(mild)(functools.partial(_gla_sample_heads, True, *args))
    pl.when(jnp.logical_not(mild))(functools.partial(_gla_sample_heads, False, *args))


def _gla_sample(q, k, v, gate, la, gon, state, nseq, tlen):
    n = q.shape[0]
    r = nseq * tlen
    assert tlen == SUB
    row = lambda w: pl.BlockSpec((r, w), lambda i: (i, 0))
    st = pl.BlockSpec((nseq, GLA_HEADS, GLA_DK, GLA_DV), lambda i: (i, 0, 0, 0))
    return pl.pallas_call(
        functools.partial(_gla_sample_kernel, nseq, tlen),
        grid=(n // r,),
        in_specs=[row(GLA_KW), row(GLA_KW), row(GLA_VW), row(GLA_VW), row(GLA_KW),
                  _const_spec((1, GLA_DV)), st],
        out_specs=[row(GLA_VW), st],
        out_shape=[jax.ShapeDtypeStruct((n, GLA_VW), BF16),
                   jax.ShapeDtypeStruct(state.shape, F32)],
        scratch_shapes=[pltpu.VMEM((r, GLA_KW), F32)],
        compiler_params=pltpu.CompilerParams(dimension_semantics=("arbitrary",),
                                             vmem_limit_bytes=VMEM_LIMIT),
        name="gla_sample",
    )(q, k, v, gate, la, gon, state)


def _out_cmlp_kernel(nchunks, mix_block, x_ref, og_ref, wo_ref, g1_ref, wi_ref, lng_ref, lnb_ref, ws_ref,
                     bs_ref, wo2_ref, gf_ref, y_ref, v_out_ref):
    x1 = x_ref[...] + _dot(og_ref[...], wo_ref[:, 0:D_MODEL])
    y_ref[...] = _drain(_cmlp_steps(x1, nchunks, mix_block, g1_ref, wi_ref, lng_ref, lnb_ref, ws_ref, bs_ref,
                                    wo2_ref, gf_ref, v_out_ref))


def _out_cmlp(x2d, og, p, ws_eff, bs_full, mix_block, nchunks):
    n = x2d.shape[0]
    rb = nchunks * CHUNK
    row = lambda w: pl.BlockSpec((rb, w), lambda i: (i, 0))
    consts = [p["wo"], p["g1"], p["wi"], p["lng"], p["lnb"], ws_eff, bs_full, p["wo2"], p["gf"]]
    return pl.pallas_call(
        functools.partial(_out_cmlp_kernel, nchunks, mix_block),
        grid=(n // rb,),
        in_specs=[row(D_MODEL), row(GLA_VW)] + [_const_spec(c.shape) for c in consts],
        out_specs=[row(D_MODEL), row(CMLP_WIDTH)],
        out_shape=[jax.ShapeDtypeStruct((n, D_MODEL), F32),
                   jax.ShapeDtypeStruct((n, CMLP_WIDTH), F32)],
        compiler_params=pltpu.CompilerParams(dimension_semantics=("arbitrary",),
                                             vmem_limit_bytes=VMEM_LIMIT),
        name="out_cmlp",
    )(x2d, og, *consts)


def kernel(x_prompt, x_sample, state_gla, norm_g, gla_w_in, gla_w_a_up, gla_b_a_up, gla_g_onorm,
           gla_w_out, cmlp_w_in, cmlp_ln_g, cmlp_ln_b, cmlp_w_spatial, cmlp_b_spatial, cmlp_w_out,
           norm_final):
    b, t, d = x_prompt.shape
    nb, nt, _ = x_sample.shape
    assert t % (PROMPT_CHUNKS_PER_STEP * CHUNK) == 0 and (nb * nt) % CHUNK == 0 and CHUNK % nt == 0

    p = {
        "g0": norm_g[0].reshape(1, d),
        "g1": norm_g[1].reshape(1, d),
        "w_in": _bf16_padded(gla_w_in[0], GLA_IN_PAD),
        "wa": jnp.pad(gla_w_a_up[0], ((0, LANE - GLA_GATE_RANK), (0, 0))).astype(BF16),
        "ba": gla_b_a_up[0].reshape(1, GLA_KW),
        "gon": gla_g_onorm[0].reshape(1, GLA_DV),
        "wo": _pad_cols_odd_tiles(gla_w_out[0]),
        "wi": _pad_cols_odd_tiles(cmlp_w_in[0]),
        "lng": cmlp_ln_g[0].reshape(1, CMLP_WIDTH),
        "lnb": cmlp_ln_b[0].reshape(1, CMLP_WIDTH),
        "wo2": _pad_cols_odd_tiles(cmlp_w_out[0]),
        "gf": norm_final.reshape(1, d),
    }
    assert p["w_in"].shape[1] // LANE % 2 == 1
    ws = cmlp_w_spatial[0]
    bs = cmlp_b_spatial[0]

    def spatial_params(c):
        reps = CHUNK // c
        ws_eff = jnp.tile(ws[:, :c, :c], (1, reps, reps))
        bs_full = jnp.repeat(jnp.tile(bs[:, :c], (1, reps)).T, CMLP_GROUP_DIM, axis=1)
        return ws_eff, bs_full

    y_prompt, s_prompt = _prompt_trunk(x_prompt, p, *spatial_params(CHUNK), PROMPT_CHUNKS_PER_STEP)

    xs = x_sample.reshape(nb * nt, d)
    q, k, v, gate, la = _gla_in(xs, p, 512)
    og, s_sample = _gla_sample(q, k, v, gate, la, p["gon"], state_gla[0], SAMPLE_SEQS_PER_STEP, nt)
    y_sample, v_rows = _out_cmlp(xs, og, p, *spatial_params(nt), nt, SAMPLE_CHUNKS_PER_STEP)

    return (y_prompt, y_sample.reshape(nb, nt, d), s_prompt[None], s_sample[None],
            v_rows.reshape(1, nb, nt, CMLP_WIDTH))
```

```python
import functools

import jax
import jax.numpy as jnp
from jax import lax
from jax.experimental import pallas as pl
from jax.experimental.pallas import tpu as pltpu

F32 = jnp.float32
BF16 = jnp.bfloat16

D_MODEL = 1024
GLA_HEADS = 4
GLA_DK = 128
GLA_DV = 256
GLA_KW = GLA_HEADS * GLA_DK
GLA_VW = GLA_HEADS * GLA_DV
GLA_GATE_RANK = 16
GLA_GATE_TAU = 16.0
CMLP_WIDTH = 1024
CMLP_GROUPS = 4
CMLP_GROUP_DIM = CMLP_WIDTH // CMLP_GROUPS
LANE = 128
CHUNK = 128
SUB = 8
PROMPT_CHUNKS_PER_STEP = 4
SAMPLE_CHUNKS_PER_STEP = 4
SAMPLE_SEQS_PER_STEP = 16
EPS = 1e-6
VMEM_LIMIT = 56 * 1024 * 1024
DECAY_GUARD = 60.0

_Q0, _K0, _V0, _G0, _A0 = 0, GLA_KW, 2 * GLA_KW, 2 * GLA_KW + GLA_VW, 2 * GLA_KW + 2 * GLA_VW
GLA_IN_PAD = _A0 + LANE


def _pad_cols_odd_tiles(w):
    tiles = -(-w.shape[1] // LANE)
    tiles += 1 - tiles % 2
    return _bf16_padded(w, tiles * LANE)


def _bf16_padded(w, cols):
    rows, have = w.shape
    return jnp.concatenate([w.astype(BF16), jnp.zeros((rows, cols - have), BF16)], axis=1)


def _dot(a, b):
    return jnp.dot(a, b, preferred_element_type=F32)


def _dot_nt(a, b):
    return lax.dot_general(a, b, (((1,), (1,)), ((), ())), preferred_element_type=F32)


def _rms(x, g):
    return x * lax.rsqrt(jnp.mean(x * x, axis=-1, keepdims=True) + EPS) * g


def _silu(x):
    return x * jax.nn.sigmoid(x)


def _neg_abs(x):
    return lax.bitcast_convert_type(
        lax.bitcast_convert_type(x, jnp.uint32) | jnp.uint32(0x80000000), F32)


def _split3(x):
    hi = x.astype(BF16)
    r1 = x - hi.astype(F32)
    mid = r1.astype(BF16)
    lo = (r1 - mid.astype(F32)).astype(BF16)
    return hi, mid, lo


def _masked_prefix_sum(mask01, x):
    hi, mid, lo = _split3(x)
    return (_dot(jnp.concatenate([mask01, mask01], axis=1), jnp.concatenate([hi, mid], axis=0))
            + _dot(mask01, lo))


def _sublane_bcast(x, row, group):
    r, n = x.shape
    x3 = x.reshape(r // group, group, n)
    return jnp.broadcast_to(x3[:, row:row + 1, :], x3.shape).reshape(r, n)


def _index_masks(r):
    row = lax.broadcasted_iota(jnp.int32, (r, r), 0)
    col = lax.broadcasted_iota(jnp.int32, (r, r), 1)
    return row ^ col, col <= row


def _log_decay(a_low, wa_ref, ba_ref):
    z = _dot(a_low.astype(BF16), wa_ref[...]) + ba_ref[...]
    log_sig = jnp.minimum(z, 0.0) - jnp.log1p(jnp.exp(-jnp.abs(z)))
    return log_sig * (1.0 / GLA_GATE_TAU)


def _layernorm(v, g, b):
    vc = v - jnp.mean(v, axis=-1, keepdims=True)
    var = jnp.mean(vc * vc, axis=-1, keepdims=True)
    return vc * lax.rsqrt(var + EPS) * g + b


def _intra_scores_factored(q, k, cum, levels, rxc, lower):
    r = q.shape[0]
    parts = []
    for s in range(SUB):
        kb = _sublane_bcast(k, s, SUB)
        cb = _sublane_bcast(cum, s, SUB)
        parts.append((q * kb * jnp.exp(jnp.minimum(cum - cb, 0.0))).astype(BF16))
    acat = jnp.concatenate(parts, axis=1)
    pr = lax.broadcasted_iota(jnp.int32, (SUB * GLA_DK, r), 0)
    pc = lax.broadcasted_iota(jnp.int32, (SUB * GLA_DK, r), 1)
    lane_sum = ((pc & (SUB - 1)) == (pr >> 7)).astype(BF16)
    scores = jnp.where((rxc < SUB) & lower, _dot(acat, lane_sum), 0.0)
    for m in levels:
        qk = jnp.concatenate([(q if (i // m) % 2 else k)[i:i + m] for i in range(0, r, m)], axis=0)
        ref = _sublane_bcast(cum, m - 1, 2 * m)
        f = (qk * jnp.exp(_neg_abs(cum - ref))).astype(BF16)
        scores = jnp.where((rxc >= m) & (rxc < 2 * m) & lower, _dot_nt(f, f), scores)
    return scores


def _intra_scores_direct(qin, k, cum, mask):
    kout = (k * jnp.exp(-cum)).astype(BF16)
    return jnp.where(mask, _dot_nt(qin, kout), 0.0)


def _out_norm_gate(o, gon, gate):
    return (_rms(o, gon) * gate).astype(BF16)


GLA_HEAD_STAGES = 3


def _gla_prompt_head(mild, h, rows, q_s, k_s, cum_s, v_s, gate_s, og_ref, s_in, s_out, gon_ref, rxc,
                     lower):
    ks = slice(h * GLA_DK, (h + 1) * GLA_DK)
    vs = slice(h * GLA_DV, (h + 1) * GLA_DV)
    q, k, cum = q_s[rows, ks], k_s[rows, ks], cum_s[rows, ks]
    vb = v_s[rows, vs]
    qin = (q * jnp.exp(cum)).astype(BF16)
    if mild:
        scores = _intra_scores_direct(qin, k, cum, lower)
    else:
        scores = _intra_scores_factored(q, k, cum, (64, 32, 16, 8), rxc, lower)
    k_t, cum_t = k.T, cum.T
    last = cum_t[:, CHUNK - 1:CHUNK]
    kd_t = (k_t * jnp.exp(last - cum_t)).astype(BF16)
    s_add = _dot(kd_t, vb)
    yield None
    state = s_in[h]
    o = _dot(jnp.concatenate([scores.astype(BF16), qin], axis=1),
             jnp.concatenate([vb, state.astype(BF16)], axis=0))
    s_out[h] = jnp.exp(last) * state + s_add
    yield None
    og_ref[rows, vs] = _out_norm_gate(o, gon_ref[...], gate_s[rows, vs])
    yield None


def _gla_prompt_chunk_stages(mild, c, *args):
    rows = slice(c * CHUNK, (c + 1) * CHUNK)
    gens = [_gla_prompt_head(mild, h, rows, *args) for h in range(GLA_HEADS)]
    for _ in range(GLA_HEAD_STAGES):
        for g in gens:
            next(g)
        yield None


def _alternate(first, second):
    last = None
    live_first = live_second = True
    while live_first or live_second:
        if live_first:
            live_first = next(first, StopIteration) is not StopIteration
        if live_second:
            item = next(second, StopIteration)
            live_second = item is not StopIteration
            last = item if live_second else last
    return last


def _cmlp_steps(x1, nchunks, mix_block, g1_ref, wi_ref, lng_ref, lnb_ref, ws_ref, bs_ref, wo2_ref,
                gf_ref, v_out_ref=None):
    w = CMLP_WIDTH
    hb = _rms(x1, g1_ref[...]).astype(BF16)
    yield None
    u = _dot(hb, wi_ref[:, 0:w])
    yield None
    vn = _layernorm(_dot(hb, wi_ref[:, w:2 * w]), lng_ref[...], lnb_ref[...])
    if v_out_ref is not None:
        v_out_ref[...] = vn
    yield None
    ug = u * _silu(_dot(hb, wi_ref[:, 2 * w:3 * w]))
    yield None
    vnb = vn.astype(BF16)
    rxc, lower = _index_masks(CHUNK)
    mix_mask = (rxc < mix_block) & lower
    zs = []
    for g in range(CMLP_GROUPS):
        gs = slice(g * CMLP_GROUP_DIM, (g + 1) * CMLP_GROUP_DIM)
        wsm = jnp.where(mix_mask, ws_ref[g], 0.0).astype(BF16)
        mixed = jnp.concatenate(
            [_dot(wsm, vnb[c * CHUNK:(c + 1) * CHUNK, gs]) + bs_ref[:, gs] for c in range(nchunks)],
            axis=0)
        zs.append((ug[:, gs] * mixed).astype(BF16))
    yield None
    acc = x1
    for g in range(CMLP_GROUPS):
        gs = slice(g * CMLP_GROUP_DIM, (g + 1) * CMLP_GROUP_DIM)
        acc = acc + _dot(zs[g], wo2_ref[gs, 0:D_MODEL])
        if g < CMLP_GROUPS - 1:
            yield None
    yield _rms(acc, gf_ref[...])


CMLP_STEPS = 9


def _drain(gen):
    out = None
    for out in gen:
        pass
    return out


def _prompt_kernel(nchunks, nblk, blk_per_seq, xa_ref, xc_ref, g0_ref, win_ref, wa_ref, ba_ref,
                   gon_ref, wo_ref, g1_ref, wi_ref, lng_ref, lnb_ref, ws_ref, bs_ref, wo2_ref, gf_ref,
                   y_ref, s_ref, q_s, k_s, cum_s, v_s, gate_s, og_s, st_s):
    j = pl.program_id(0)
    valid = j < nblk
    jj = jnp.minimum(j, nblk - 1)
    is_first = lax.rem(jj, blk_per_seq) == 0
    slot = lax.rem(j, 2)
    og_cur, og_prev = og_s.at[slot], og_s.at[1 - slot]

    @pl.when(j == 0)
    def _():
        og_s[...] = jnp.zeros_like(og_s)

    @pl.when(is_first)
    def _():
        s_ref[...] = jnp.zeros_like(s_ref)

    x1 = xc_ref[0] + _dot(og_prev[...], wo_ref[:, 0:D_MODEL])
    hb = _rms(xa_ref[0], g0_ref[...]).astype(BF16)
    la = _log_decay(_dot(hb, win_ref[:, _A0:GLA_IN_PAD]), wa_ref, ba_ref)
    q_s[...] = _dot(hb, win_ref[:, _Q0:_K0]) * (GLA_DK ** -0.5)
    k_s[...] = _dot(hb, win_ref[:, _K0:_V0])
    rxc, lower = _index_masks(CHUNK)
    tri = lower.astype(BF16)
    total = None
    for c in range(nchunks):
        rows = slice(c * CHUNK, (c + 1) * CHUNK)
        cum = _masked_prefix_sum(tri, la[rows])
        cum_s[rows, :] = cum
        end = jnp.min(cum[CHUNK - SUB:CHUNK, :])
        total = end if total is None else jnp.minimum(total, end)
    mild = total > -DECAY_GUARD
    cmlp = _cmlp_steps(x1, nchunks, CHUNK, g1_ref, wi_ref, lng_ref, lnb_ref, ws_ref, bs_ref, wo2_ref,
                       gf_ref)
    next(cmlp)
    v_s[...] = _dot(hb, win_ref[:, _V0:_G0]).astype(BF16)
    gate_s[...] = _silu(_dot(hb, win_ref[:, _G0:_A0]))

    def recurrence(is_mild, carried):
        for c in range(nchunks):
            yield from _gla_prompt_chunk_stages(
                is_mild, c, q_s, k_s, cum_s, v_s, gate_s, og_cur,
                s_ref.at[0] if c == 0 else carried, carried, gon_ref, rxc, lower)

    y_ref[0] = _alternate(recurrence(True, st_s), cmlp)
    commit = jnp.logical_and(valid, mild)
    for h in range(GLA_HEADS):
        s_ref[0, h] = jnp.where(commit, st_s[h], s_ref[0, h])

    @pl.when(jnp.logical_and(valid, jnp.logical_not(mild)))
    def _():
        _drain(recurrence(False, s_ref.at[0]))


def _const_spec(shape):
    nd = len(shape)
    return pl.BlockSpec(shape, lambda *_: (0,) * nd, pipeline_mode=pl.Buffered(1))


def _prompt_trunk(x, p, ws_eff, bs_full, nchunks):
    b, t, d = x.shape
    rb = nchunks * CHUNK
    bps = t // rb
    nblk = b * bps
    consts = [p["g0"], p["w_in"], p["wa"], p["ba"], p["gon"], p["wo"], p["g1"], p["wi"], p["lng"],
              p["lnb"], ws_eff, bs_full, p["wo2"], p["gf"]]

    def cur(j):
        jj = jnp.minimum(j, nblk - 1)
        return jj // bps, jj % bps

    def prev(j):
        jj = jnp.maximum(j - 1, 0)
        return jj // bps, jj % bps

    return pl.pallas_call(
        functools.partial(_prompt_kernel, nchunks, nblk, bps),
        grid=(nblk + 1,),
        in_specs=[pl.BlockSpec((1, rb, d), lambda j: (*cur(j), 0)),
                  pl.BlockSpec((1, rb, d), lambda j: (*prev(j), 0))]
        + [_const_spec(c.shape) for c in consts],
        out_specs=[pl.BlockSpec((1, rb, d), lambda j: (*prev(j), 0)),
                   pl.BlockSpec((1, GLA_HEADS, GLA_DK, GLA_DV), lambda j: (cur(j)[0], 0, 0, 0))],
        out_shape=[jax.ShapeDtypeStruct((b, t, d), F32),
                   jax.ShapeDtypeStruct((b, GLA_HEADS, GLA_DK, GLA_DV), F32)],
        scratch_shapes=[pltpu.VMEM((rb, GLA_KW), F32), pltpu.VMEM((rb, GLA_KW), F32),
                        pltpu.VMEM((rb, GLA_KW), F32), pltpu.VMEM((rb, GLA_VW), BF16),
                        pltpu.VMEM((rb, GLA_VW), F32), pltpu.VMEM((2, rb, GLA_VW), BF16),
                        pltpu.VMEM((GLA_HEADS, GLA_DK, GLA_DV), F32)],
        compiler_params=pltpu.CompilerParams(dimension_semantics=("arbitrary",),
                                             vmem_limit_bytes=VMEM_LIMIT),
        name="prompt_trunk",
    )(x, x, *consts)


def _gla_in_kernel(x_ref, g_ref, w_ref, wa_ref, ba_ref, q_ref, k_ref, v_ref, gate_ref, la_ref):
    hb = _rms(x_ref[...], g_ref[...]).astype(BF16)
    q_ref[...] = _dot(hb, w_ref[:, _Q0:_K0]) * (GLA_DK ** -0.5)
    k_ref[...] = _dot(hb, w_ref[:, _K0:_V0])
    v_ref[...] = _dot(hb, w_ref[:, _V0:_G0]).astype(BF16)
    gate_ref[...] = _silu(_dot(hb, w_ref[:, _G0:_A0]))
    la_ref[...] = _log_decay(_dot(hb, w_ref[:, _A0:GLA_IN_PAD]), wa_ref, ba_ref)


def _gla_in(x2d, p, tm):
    n = x2d.shape[0]
    row = lambda w: pl.BlockSpec((tm, w), lambda i: (i, 0))
    consts = [p["g0"], p["w_in"], p["wa"], p["ba"]]
    return pl.pallas_call(
        _gla_in_kernel,
        grid=(n // tm,),
        in_specs=[row(D_MODEL)] + [_const_spec(c.shape) for c in consts],
        out_specs=[row(GLA_KW), row(GLA_KW), row(GLA_VW), row(GLA_VW), row(GLA_KW)],
        out_shape=[jax.ShapeDtypeStruct((n, GLA_KW), F32), jax.ShapeDtypeStruct((n, GLA_KW), F32),
                   jax.ShapeDtypeStruct((n, GLA_VW), BF16), jax.ShapeDtypeStruct((n, GLA_VW), F32),
                   jax.ShapeDtypeStruct((n, GLA_KW), F32)],
        compiler_params=pltpu.CompilerParams(dimension_semantics=("arbitrary",),
                                             vmem_limit_bytes=VMEM_LIMIT),
        name="gla_in",
    )(x2d, *consts)


def _gla_sample_heads(mild, nseq, tlen, q_ref, k_ref, v_ref, gate_ref, cum_s, gon_ref, s_in_ref,
                      og_ref, s_out_ref, rxc, lower):
    r = nseq * tlen
    block_mask = (rxc < tlen) & lower
    seq_of_lane = lax.broadcasted_iota(jnp.int32, (GLA_DK, r), 1) >> 3
    for h in range(GLA_HEADS):
        ks = slice(h * GLA_DK, (h + 1) * GLA_DK)
        vs = slice(h * GLA_DV, (h + 1) * GLA_DV)
        q, k, cum = q_ref[:, ks], k_ref[:, ks], cum_s[:, ks]
        vb = v_ref[:, vs]
        qin = (q * jnp.exp(cum)).astype(BF16)
        if mild:
            scores = _intra_scores_direct(qin, k, cum, block_mask)
        else:
            scores = _intra_scores_factored(q, k, cum, (), rxc, lower)
        o_intra = _dot(scores.astype(BF16), vb)
        kd_t = (k * jnp.exp(_sublane_bcast(cum, tlen - 1, tlen) - cum)).T
        cum_t = cum.T
        for i in range(nseq):
            rs = slice(i * tlen, (i + 1) * tlen)
            state = s_in_ref[i, h]
            o = o_intra[rs] + _dot(qin[rs], state.astype(BF16))
            og_ref[rs, vs] = _out_norm_gate(o, gon_ref[...], gate_ref[rs, vs])
            last = cum_t[:, (i + 1) * tlen - 1:(i + 1) * tlen]
            kd_i = jnp.where(seq_of_lane == i, kd_t, 0.0).astype(BF16)
            s_out_ref[i, h] = jnp.exp(last) * state + _dot(kd_i, vb)


def _gla_sample_kernel(nseq, tlen, q_ref, k_ref, v_ref, gate_ref, la_ref, gon_ref, s_in_ref,
                       og_ref, s_out_ref, cum_s):
    rxc, lower = _index_masks(nseq * tlen)
    cum = _masked_prefix_sum(((rxc < tlen) & lower).astype(BF16), la_ref[...])
    cum_s[...] = cum
    mild = jnp.min(cum) > -DECAY_GUARD
    args = (nseq, tlen, q_ref, k_ref, v_ref, gate_ref, cum_s, gon_ref, s_in_ref, og_ref, s_out_ref,
            rxc, lower)
    pl.when(mild)(functools.partial(_gla_sample_heads, True, *args))
    pl.when(jnp.logical_not(mild))(functools.partial(_gla_sample_heads, False, *args))


def _gla_sample(q, k, v, gate, la, gon, state, nseq, tlen):
    n = q.shape[0]
    r = nseq * tlen
    assert tlen == SUB
    row = lambda w: pl.BlockSpec((r, w), lambda i: (i, 0))
    st = pl.BlockSpec((nseq, GLA_HEADS, GLA_DK, GLA_DV), lambda i: (i, 0, 0, 0))
    return pl.pallas_call(
        functools.partial(_gla_sample_kernel, nseq, tlen),
        grid=(n // r,),
        in_specs=[row(GLA_KW), row(GLA_KW), row(GLA_VW), row(GLA_VW), row(GLA_KW),
                  _const_spec((1, GLA_DV)), st],
        out_specs=[row(GLA_VW), st],
        out_shape=[jax.ShapeDtypeStruct((n, GLA_VW), BF16),
                   jax.ShapeDtypeStruct(state.shape, F32)],
        scratch_shapes=[pltpu.VMEM((r, GLA_KW), F32)],
        compiler_params=pltpu.CompilerParams(dimension_semantics=("arbitrary",),
                                             vmem_limit_bytes=VMEM_LIMIT),
        name="gla_sample",
    )(q, k, v, gate, la, gon, state)


def _out_cmlp_kernel(nchunks, mix_block, x_ref, og_ref, wo_ref, g1_ref, wi_ref, lng_ref, lnb_ref, ws_ref,
                     bs_ref, wo2_ref, gf_ref, y_ref, v_out_ref):
    x1 = x_ref[...] + _dot(og_ref[...], wo_ref[:, 0:D_MODEL])
    y_ref[...] = _drain(_cmlp_steps(x1, nchunks, mix_block, g1_ref, wi_ref, lng_ref, lnb_ref, ws_ref, bs_ref,
                                    wo2_ref, gf_ref, v_out_ref))


def _out_cmlp(x2d, og, p, ws_eff, bs_full, mix_block, nchunks):
    n = x2d.shape[0]
    rb = nchunks * CHUNK
    row = lambda w: pl.BlockSpec((rb, w), lambda i: (i, 0))
    consts = [p["wo"], p["g1"], p["wi"], p["lng"], p["lnb"], ws_eff, bs_full, p["wo2"], p["gf"]]
    return pl.pallas_call(
        functools.partial(_out_cmlp_kernel, nchunks, mix_block),
        grid=(n // rb,),
        in_specs=[row(D_MODEL), row(GLA_VW)] + [_const_spec(c.shape) for c in consts],
        out_specs=[row(D_MODEL), row(CMLP_WIDTH)],
        out_shape=[jax.ShapeDtypeStruct((n, D_MODEL), F32),
                   jax.ShapeDtypeStruct((n, CMLP_WIDTH), F32)],
        compiler_params=pltpu.CompilerParams(dimension_semantics=("arbitrary",),
                                             vmem_limit_bytes=VMEM_LIMIT),
        name="out_cmlp",
    )(x2d, og, *consts)


def kernel(x_prompt, x_sample, state_gla, norm_g, gla_w_in, gla_w_a_up, gla_b_a_up, gla_g_onorm,
           gla_w_out, cmlp_w_in, cmlp_ln_g, cmlp_ln_b, cmlp_w_spatial, cmlp_b_spatial, cmlp_w_out,
           norm_final):
    b, t, d = x_prompt.shape
    nb, nt, _ = x_sample.shape
    assert t % (PROMPT_CHUNKS_PER_STEP * CHUNK) == 0 and (nb * nt) % CHUNK == 0 and CHUNK % nt == 0

    p = {
        "g0": norm_g[0].reshape(1, d),
        "g1": norm_g[1].reshape(1, d),
        "w_in": _bf16_padded(gla_w_in[0], GLA_IN_PAD),
        "wa": jnp.pad(gla_w_a_up[0], ((0, LANE - GLA_GATE_RANK), (0, 0))).astype(BF16),
        "ba": gla_b_a_up[0].reshape(1, GLA_KW),
        "gon": gla_g_onorm[0].reshape(1, GLA_DV),
        "wo": _pad_cols_odd_tiles(gla_w_out[0]),
        "wi": _pad_cols_odd_tiles(cmlp_w_in[0]),
        "lng": cmlp_ln_g[0].reshape(1, CMLP_WIDTH),
        "lnb": cmlp_ln_b[0].reshape(1, CMLP_WIDTH),
        "wo2": _pad_cols_odd_tiles(cmlp_w_out[0]),
        "gf": norm_final.reshape(1, d),
    }
    assert p["w_in"].shape[1] // LANE % 2 == 1
    ws = cmlp_w_spatial[0]
    bs = cmlp_b_spatial[0]

    def spatial_params(c):
        reps = CHUNK // c
        ws_eff = jnp.tile(ws[:, :c, :c], (1, reps, reps))
        bs_full = jnp.repeat(jnp.tile(bs[:, :c], (1, reps)).T, CMLP_GROUP_DIM, axis=1)
        return ws_eff, bs_full

    y_prompt, s_prompt = _prompt_trunk(x_prompt, p, *spatial_params(CHUNK), PROMPT_CHUNKS_PER_STEP)

    xs = x_sample.reshape(nb * nt, d)
    q, k, v, gate, la = _gla_in(xs, p, 512)
    og, s_sample = _gla_sample(q, k, v, gate, la, p["gon"], state_gla[0], SAMPLE_SEQS_PER_STEP, nt)
    y_sample, v_rows = _out_cmlp(xs, og, p, *spatial_params(nt), nt, SAMPLE_CHUNKS_PER_STEP)

    return (y_prompt, y_sample.reshape(nb, nt, d), s_prompt[None], s_sample[None],
            v_rows.reshape(1, nb, nt, CMLP_WIDTH))
```

```python
import functools

import jax
import jax.numpy as jnp
from jax import lax
from jax.experimental import pallas as pl
from jax.experimental.pallas import tpu as pltpu

F32 = jnp.float32
BF16 = jnp.bfloat16

D_MODEL = 1024
GLA_HEADS = 4
GLA_DK = 128
GLA_DV = 256
GLA_KW = GLA_HEADS * GLA_DK
GLA_VW = GLA_HEADS * GLA_DV
GLA_GATE_RANK = 16
GLA_GATE_TAU = 16.0
CMLP_WIDTH = 1024
CMLP_GROUPS = 4
CMLP_GROUP_DIM = CMLP_WIDTH // CMLP_GROUPS
LANE = 128
CHUNK = 128
SUB = 8
PROMPT_CHUNKS_PER_STEP = 4
SAMPLE_CHUNKS_PER_STEP = 4
SAMPLE_SEQS_PER_STEP = 16
EPS = 1e-6
VMEM_LIMIT = 56 * 1024 * 1024
DECAY_GUARD = 60.0

_Q0, _K0, _V0, _G0, _A0 = 0, GLA_KW, 2 * GLA_KW, 2 * GLA_KW + GLA_VW, 2 * GLA_KW + 2 * GLA_VW
GLA_IN_PAD = _A0 + LANE


def _odd_tile_cols(cols):
    tiles = -(-cols // LANE)
    return (tiles + 1 - tiles % 2) * LANE


def _stage_weights_kernel(*refs):
    n = len(refs) // 2
    for src, dst in zip(refs[:n], refs[n:]):
        cols = src.shape[1]
        dst[:, 0:cols] = src[...].astype(BF16)
        if dst.shape[1] > cols:
            dst[:, cols:] = jnp.zeros((dst.shape[0], dst.shape[1] - cols), BF16)


def _stage_weights(*weights):
    rows = weights[0].shape[0]
    assert all(w.shape[0] == rows for w in weights)
    outs = [_odd_tile_cols(w.shape[1]) for w in weights]
    blk = lambda c: pl.BlockSpec((CHUNK, c), lambda i: (i, 0))
    return pl.pallas_call(
        _stage_weights_kernel,
        grid=(rows // CHUNK,),
        in_specs=[blk(w.shape[1]) for w in weights],
        out_specs=[blk(c) for c in outs],
        out_shape=[jax.ShapeDtypeStruct((rows, c), BF16) for c in outs],
        compiler_params=pltpu.CompilerParams(dimension_semantics=("arbitrary",),
                                             vmem_limit_bytes=VMEM_LIMIT),
        name="stage_weights",
    )(*weights)


def _dot(a, b):
    return jnp.dot(a, b, preferred_element_type=F32)


def _dot_nt(a, b):
    return lax.dot_general(a, b, (((1,), (1,)), ((), ())), preferred_element_type=F32)


def _rms(x, g):
    return x * lax.rsqrt(jnp.mean(x * x, axis=-1, keepdims=True) + EPS) * g


def _silu(x):
    return x * jax.nn.sigmoid(x)


def _neg_abs(x):
    return lax.bitcast_convert_type(
        lax.bitcast_convert_type(x, jnp.uint32) | jnp.uint32(0x80000000), F32)


def _split3(x):
    hi = x.astype(BF16)
    r1 = x - hi.astype(F32)
    mid = r1.astype(BF16)
    lo = (r1 - mid.astype(F32)).astype(BF16)
    return hi, mid, lo


def _masked_prefix_sum(mask01, x):
    hi, mid, lo = _split3(x)
    return (_dot(jnp.concatenate([mask01, mask01], axis=1), jnp.concatenate([hi, mid], axis=0))
            + _dot(mask01, lo))


def _sublane_bcast(x, row, group):
    r, n = x.shape
    x3 = x.reshape(r // group, group, n)
    return jnp.broadcast_to(x3[:, row:row + 1, :], x3.shape).reshape(r, n)


def _index_masks(r):
    row = lax.broadcasted_iota(jnp.int32, (r, r), 0)
    col = lax.broadcasted_iota(jnp.int32, (r, r), 1)
    return row ^ col, col <= row


def _log_decay(a_low, wa_ref, ba_ref):
    z = _dot(a_low.astype(BF16), wa_ref[...]) + ba_ref[...]
    log_sig = jnp.minimum(z, 0.0) - jnp.log1p(jnp.exp(-jnp.abs(z)))
    return log_sig * (1.0 / GLA_GATE_TAU)


def _layernorm(v, g, b):
    vc = v - jnp.mean(v, axis=-1, keepdims=True)
    var = jnp.mean(vc * vc, axis=-1, keepdims=True)
    return vc * lax.rsqrt(var + EPS) * g + b


def _intra_scores_factored(q, k, cum, levels, rxc, lower):
    r = q.shape[0]
    parts = []
    for s in range(SUB):
        kb = _sublane_bcast(k, s, SUB)
        cb = _sublane_bcast(cum, s, SUB)
        parts.append((q * kb * jnp.exp(jnp.minimum(cum - cb, 0.0))).astype(BF16))
    acat = jnp.concatenate(parts, axis=1)
    pr = lax.broadcasted_iota(jnp.int32, (SUB * GLA_DK, r), 0)
    pc = lax.broadcasted_iota(jnp.int32, (SUB * GLA_DK, r), 1)
    lane_sum = ((pc & (SUB - 1)) == (pr >> 7)).astype(BF16)
    scores = jnp.where((rxc < SUB) & lower, _dot(acat, lane_sum), 0.0)
    for m in levels:
        qk = jnp.concatenate([(q if (i // m) % 2 else k)[i:i + m] for i in range(0, r, m)], axis=0)
        ref = _sublane_bcast(cum, m - 1, 2 * m)
        f = (qk * jnp.exp(_neg_abs(cum - ref))).astype(BF16)
        scores = jnp.where((rxc >= m) & (rxc < 2 * m) & lower, _dot_nt(f, f), scores)
    return scores


def _intra_scores_direct(qin, k, cum, mask):
    kout = (k * jnp.exp(-cum)).astype(BF16)
    return jnp.where(mask, _dot_nt(qin, kout), 0.0)


def _out_norm_gate(o, gon, gate):
    return (_rms(o, gon) * gate).astype(BF16)


GLA_HEAD_STAGES = 3


def _gla_prompt_head(mild, h, rows, q_s, k_s, cum_s, v_s, gate_s, og_ref, s_in, s_out, gon_ref, rxc,
                     lower):
    ks = slice(h * GLA_DK, (h + 1) * GLA_DK)
    vs = slice(h * GLA_DV, (h + 1) * GLA_DV)
    q, k, cum = q_s[rows, ks], k_s[rows, ks], cum_s[rows, ks]
    vb = v_s[rows, vs]
    qin = (q * jnp.exp(cum)).astype(BF16)
    if mild:
        scores = _intra_scores_direct(qin, k, cum, lower)
    else:
        scores = _intra_scores_factored(q, k, cum, (64, 32, 16, 8), rxc, lower)
    k_t, cum_t = k.T, cum.T
    last = cum_t[:, CHUNK - 1:CHUNK]
    kd_t = (k_t * jnp.exp(last - cum_t)).astype(BF16)
    s_add = _dot(kd_t, vb)
    yield None
    state = s_in[h]
    o = _dot(jnp.concatenate([scores.astype(BF16), qin], axis=1),
             jnp.concatenate([vb, state.astype(BF16)], axis=0))
    s_out[h] = jnp.exp(last) * state + s_add
    yield None
    og_ref[rows, vs] = _out_norm_gate(o, gon_ref[...], gate_s[rows, vs])
    yield None


def _gla_prompt_chunk_stages(mild, c, *args):
    rows = slice(c * CHUNK, (c + 1) * CHUNK)
    gens = [_gla_prompt_head(mild, h, rows, *args) for h in range(GLA_HEADS)]
    for _ in range(GLA_HEAD_STAGES):
        for g in gens:
            next(g)
        yield None


def _alternate(first, second):
    last = None
    live_first = live_second = True
    while live_first or live_second:
        if live_first:
            live_first = next(first, StopIteration) is not StopIteration
        if live_second:
            item = next(second, StopIteration)
            live_second = item is not StopIteration
            last = item if live_second else last
    return last


def _cmlp_steps(x1, nchunks, mix_block, g1_ref, wi_ref, lng_ref, lnb_ref, ws_ref, bs_ref, wo2_ref,
                gf_ref, v_out_ref=None):
    w = CMLP_WIDTH
    hb = _rms(x1, g1_ref[...]).astype(BF16)
    yield None
    u = _dot(hb, wi_ref[:, 0:w])
    yield None
    vn = _layernorm(_dot(hb, wi_ref[:, w:2 * w]), lng_ref[...], lnb_ref[...])
    if v_out_ref is not None:
        v_out_ref[...] = vn
    yield None
    ug = u * _silu(_dot(hb, wi_ref[:, 2 * w:3 * w]))
    yield None
    vnb = vn.astype(BF16)
    rxc, lower = _index_masks(CHUNK)
    mix_mask = (rxc < mix_block) & lower
    zs = []
    for g in range(CMLP_GROUPS):
        gs = slice(g * CMLP_GROUP_DIM, (g + 1) * CMLP_GROUP_DIM)
        wsm = jnp.where(mix_mask, ws_ref[g], 0.0).astype(BF16)
        mixed = jnp.concatenate(
            [_dot(wsm, vnb[c * CHUNK:(c + 1) * CHUNK, gs]) + bs_ref[:, gs] for c in range(nchunks)],
            axis=0)
        zs.append((ug[:, gs] * mixed).astype(BF16))
    yield None
    acc = x1
    for g in range(CMLP_GROUPS):
        gs = slice(g * CMLP_GROUP_DIM, (g + 1) * CMLP_GROUP_DIM)
        acc = acc + _dot(zs[g], wo2_ref[gs, 0:D_MODEL])
        if g < CMLP_GROUPS - 1:
            yield None
    yield _rms(acc, gf_ref[...])


CMLP_STEPS = 9


def _drain(gen):
    out = None
    for out in gen:
        pass
    return out


def _prompt_kernel(nchunks, nblk, blk_per_seq, xa_ref, xc_ref, g0_ref, win_ref, wa_ref, ba_ref,
                   gon_ref, wo_ref, g1_ref, wi_ref, lng_ref, lnb_ref, ws_ref, bs_ref, wo2_ref, gf_ref,
                   y_ref, s_ref, q_s, k_s, cum_s, v_s, gate_s, og_s, st_s):
    j = pl.program_id(0)
    valid = j < nblk
    jj = jnp.minimum(j, nblk - 1)
    is_first = lax.rem(jj, blk_per_seq) == 0
    slot = lax.rem(j, 2)
    og_cur, og_prev = og_s.at[slot], og_s.at[1 - slot]

    @pl.when(j == 0)
    def _():
        og_s[...] = jnp.zeros_like(og_s)

    @pl.when(is_first)
    def _():
        s_ref[...] = jnp.zeros_like(s_ref)

    x1 = xc_ref[0] + _dot(og_prev[...], wo_ref[:, 0:D_MODEL])
    hb = _rms(xa_ref[0], g0_ref[...]).astype(BF16)
    la = _log_decay(_dot(hb, win_ref[:, _A0:GLA_IN_PAD]), wa_ref, ba_ref)
    q_s[...] = _dot(hb, win_ref[:, _Q0:_K0]) * (GLA_DK ** -0.5)
    k_s[...] = _dot(hb, win_ref[:, _K0:_V0])
    rxc, lower = _index_masks(CHUNK)
    tri = lower.astype(BF16)
    total = None
    for c in range(nchunks):
        rows = slice(c * CHUNK, (c + 1) * CHUNK)
        cum = _masked_prefix_sum(tri, la[rows])
        cum_s[rows, :] = cum
        end = jnp.min(cum[CHUNK - SUB:CHUNK, :])
        total = end if total is None else jnp.minimum(total, end)
    mild = total > -DECAY_GUARD
    cmlp = _cmlp_steps(x1, nchunks, CHUNK, g1_ref, wi_ref, lng_ref, lnb_ref, ws_ref, bs_ref, wo2_ref,
                       gf_ref)
    next(cmlp)
    v_s[...] = _dot(hb, win_ref[:, _V0:_G0]).astype(BF16)
    gate_s[...] = _silu(_dot(hb, win_ref[:, _G0:_A0]))

    def recurrence(is_mild, carried):
        for c in range(nchunks):
            yield from _gla_prompt_chunk_stages(
                is_mild, c, q_s, k_s, cum_s, v_s, gate_s, og_cur,
                s_ref.at[0] if c == 0 else carried, carried, gon_ref, rxc, lower)

    y_ref[0] = _alternate(recurrence(True, st_s), cmlp)
    commit = jnp.logical_and(valid, mild)
    for h in range(GLA_HEADS):
        s_ref[0, h] = jnp.where(commit, st_s[h], s_ref[0, h])

    @pl.when(jnp.logical_and(valid, jnp.logical_not(mild)))
    def _():
        _drain(recurrence(False, s_ref.at[0]))


def _const_spec(shape):
    nd = len(shape)
    return pl.BlockSpec(shape, lambda *_: (0,) * nd, pipeline_mode=pl.Buffered(1))


def _prompt_trunk(x, p, ws_eff, bs_full, nchunks):
    b, t, d = x.shape
    rb = nchunks * CHUNK
    bps = t // rb
    nblk = b * bps
    consts = [p["g0"], p["w_in"], p["wa"], p["ba"], p["gon"], p["wo"], p["g1"], p["wi"], p["lng"],
              p["lnb"], ws_eff, bs_full, p["wo2"], p["gf"]]

    def cur(j):
        jj = jnp.minimum(j, nblk - 1)
        return jj // bps, jj % bps

    def prev(j):
        jj = jnp.maximum(j - 1, 0)
        return jj // bps, jj % bps

    return pl.pallas_call(
        functools.partial(_prompt_kernel, nchunks, nblk, bps),
        grid=(nblk + 1,),
        in_specs=[pl.BlockSpec((1, rb, d), lambda j: (*cur(j), 0)),
                  pl.BlockSpec((1, rb, d), lambda j: (*prev(j), 0))]
        + [_const_spec(c.shape) for c in consts],
        out_specs=[pl.BlockSpec((1, rb, d), lambda j: (*prev(j), 0)),
                   pl.BlockSpec((1, GLA_HEADS, GLA_DK, GLA_DV), lambda j: (cur(j)[0], 0, 0, 0))],
        out_shape=[jax.ShapeDtypeStruct((b, t, d), F32),
                   jax.ShapeDtypeStruct((b, GLA_HEADS, GLA_DK, GLA_DV), F32)],
        scratch_shapes=[pltpu.VMEM((rb, GLA_KW), F32), pltpu.VMEM((rb, GLA_KW), F32),
                        pltpu.VMEM((rb, GLA_KW), F32), pltpu.VMEM((rb, GLA_VW), BF16),
                        pltpu.VMEM((rb, GLA_VW), F32), pltpu.VMEM((2, rb, GLA_VW), BF16),
                        pltpu.VMEM((GLA_HEADS, GLA_DK, GLA_DV), F32)],
        compiler_params=pltpu.CompilerParams(dimension_semantics=("arbitrary",),
                                             vmem_limit_bytes=VMEM_LIMIT),
        name="prompt_trunk",
    )(x, x, *consts)


def _gla_in_kernel(x_ref, g_ref, w_ref, wa_ref, ba_ref, q_ref, k_ref, v_ref, gate_ref, la_ref):
    hb = _rms(x_ref[...], g_ref[...]).astype(BF16)
    q_ref[...] = _dot(hb, w_ref[:, _Q0:_K0]) * (GLA_DK ** -0.5)
    k_ref[...] = _dot(hb, w_ref[:, _K0:_V0])
    v_ref[...] = _dot(hb, w_ref[:, _V0:_G0]).astype(BF16)
    gate_ref[...] = _silu(_dot(hb, w_ref[:, _G0:_A0]))
    la_ref[...] = _log_decay(_dot(hb, w_ref[:, _A0:GLA_IN_PAD]), wa_ref, ba_ref)


def _gla_in(x2d, p, tm):
    n = x2d.shape[0]
    row = lambda w: pl.BlockSpec((tm, w), lambda i: (i, 0))
    consts = [p["g0"], p["w_in"], p["wa"], p["ba"]]
    return pl.pallas_call(
        _gla_in_kernel,
        grid=(n // tm,),
        in_specs=[row(D_MODEL)] + [_const_spec(c.shape) for c in consts],
        out_specs=[row(GLA_KW), row(GLA_KW), row(GLA_VW), row(GLA_VW), row(GLA_KW)],
        out_shape=[jax.ShapeDtypeStruct((n, GLA_KW), F32), jax.ShapeDtypeStruct((n, GLA_KW), F32),
                   jax.ShapeDtypeStruct((n, GLA_VW), BF16), jax.ShapeDtypeStruct((n, GLA_VW), F32),
                   jax.ShapeDtypeStruct((n, GLA_KW), F32)],
        compiler_params=pltpu.CompilerParams(dimension_semantics=("arbitrary",),
                                             vmem_limit_bytes=VMEM_LIMIT),
        name="gla_in",
    )(x2d, *consts)


def _gla_sample_heads(mild, nseq, tlen, q_ref, k_ref, v_ref, gate_ref, cum_s, gon_ref, s_in_ref,
                      og_ref, s_out_ref, rxc, lower):
    r = nseq * tlen
    block_mask = (rxc < tlen) & lower
    seq_of_lane = lax.broadcasted_iota(jnp.int32, (GLA_DK, r), 1) >> 3
    for h in range(GLA_HEADS):
        ks = slice(h * GLA_DK, (h + 1) * GLA_DK)
        vs = slice(h * GLA_DV, (h + 1) * GLA_DV)
        q, k, cum = q_ref[:, ks], k_ref[:, ks], cum_s[:, ks]
        vb = v_ref[:, vs]
        qin = (q * jnp.exp(cum)).astype(BF16)
        if mild:
            scores = _intra_scores_direct(qin, k, cum, block_mask)
        else:
            scores = _intra_scores_factored(q, k, cum, (), rxc, lower)
        o_intra = _dot(scores.astype(BF16), vb)
        kd_t = (k * jnp.exp(_sublane_bcast(cum, tlen - 1, tlen) - cum)).T
        cum_t = cum.T
        for i in range(nseq):
            rs = slice(i * tlen, (i + 1) * tlen)
            state = s_in_ref[i, h]
            o = o_intra[rs] + _dot(qin[rs], state.astype(BF16))
            og_ref[rs, vs] = _out_norm_gate(o, gon_ref[...], gate_ref[rs, vs])
            last = cum_t[:, (i + 1) * tlen - 1:(i + 1) * tlen]
            kd_i = jnp.where(seq_of_lane == i, kd_t, 0.0).astype(BF16)
            s_out_ref[i, h] = jnp.exp(last) * state + _dot(kd_i, vb)


def _gla_sample_kernel(nseq, tlen, q_ref, k_ref, v_ref, gate_ref, la_ref, gon_ref, s_in_ref,
                       og_ref, s_out_ref, cum_s):
    rxc, lower = _index_masks(nseq * tlen)
    cum = _masked_prefix_sum(((rxc < tlen) & lower).astype(BF16), la_ref[...])
    cum_s[...] = cum
    mild = jnp.min(cum) > -DECAY_GUARD
    args = (nseq, tlen, q_ref, k_ref, v_ref, gate_ref, cum_s, gon_ref, s_in_ref, og_ref, s_out_ref,
            rxc, lower)
    pl.when(mild)(functools.partial(_gla_sample_heads, True, *args))
    pl.when(jnp.logical_not(mild))(functools.partial(_gla_sample_heads, False, *args))


def _gla_sample(q, k, v, gate, la, gon, state, nseq, tlen):
    n = q.shape[0]
    r = nseq * tlen
    assert tlen == SUB
    row = lambda w: pl.BlockSpec((r, w), lambda i: (i, 0))
    st = pl.BlockSpec((nseq, GLA_HEADS, GLA_DK, GLA_DV), lambda i: (i, 0, 0, 0))
    return pl.pallas_call(
        functools.partial(_gla_sample_kernel, nseq, tlen),
        grid=(n // r,),
        in_specs=[row(GLA_KW), row(GLA_KW), row(GLA_VW), row(GLA_VW), row(GLA_KW),
                  _const_spec((1, GLA_DV)), st],
        out_specs=[row(GLA_VW), st],
        out_shape=[jax.ShapeDtypeStruct((n, GLA_VW), BF16),
                   jax.ShapeDtypeStruct(state.shape, F32)],
        scratch_shapes=[pltpu.VMEM((r, GLA_KW), F32)],
        compiler_params=pltpu.CompilerParams(dimension_semantics=("arbitrary",),
                                             vmem_limit_bytes=VMEM_LIMIT),
        name="gla_sample",
    )(q, k, v, gate, la, gon, state)


def _out_cmlp_kernel(nchunks, mix_block, x_ref, og_ref, wo_ref, g1_ref, wi_ref, lng_ref, lnb_ref, ws_ref,
                     bs_ref, wo2_ref, gf_ref, y_ref, v_out_ref):
    x1 = x_ref[...] + _dot(og_ref[...], wo_ref[:, 0:D_MODEL])
    y_ref[...] = _drain(_cmlp_steps(x1, nchunks, mix_block, g1_ref, wi_ref, lng_ref, lnb_ref, ws_ref, bs_ref,
                                    wo2_ref, gf_ref, v_out_ref))


def _out_cmlp(x2d, og, p, ws_eff, bs_full, mix_block, nchunks):
    n = x2d.shape[0]
    rb = nchunks * CHUNK
    row = lambda w: pl.BlockSpec((rb, w), lambda i: (i, 0))
    consts = [p["wo"], p["g1"], p["wi"], p["lng"], p["lnb"], ws_eff, bs_full, p["wo2"], p["gf"]]
    return pl.pallas_call(
        functools.partial(_out_cmlp_kernel, nchunks, mix_block),
        grid=(n // rb,),
        in_specs=[row(D_MODEL), row(GLA_VW)] + [_const_spec(c.shape) for c in consts],
        out_specs=[row(D_MODEL), row(CMLP_WIDTH)],
        out_shape=[jax.ShapeDtypeStruct((n, D_MODEL), F32),
                   jax.ShapeDtypeStruct((n, CMLP_WIDTH), F32)],
        compiler_params=pltpu.CompilerParams(dimension_semantics=("arbitrary",),
                                             vmem_limit_bytes=VMEM_LIMIT),
        name="out_cmlp",
    )(x2d, og, *consts)


def kernel(x_prompt, x_sample, state_gla, norm_g, gla_w_in, gla_w_a_up, gla_b_a_up, gla_g_onorm,
           gla_w_out, cmlp_w_in, cmlp_ln_g, cmlp_ln_b, cmlp_w_spatial, cmlp_b_spatial, cmlp_w_out,
           norm_final):
    b, t, d = x_prompt.shape
    nb, nt, _ = x_sample.shape
    assert t % (PROMPT_CHUNKS_PER_STEP * CHUNK) == 0 and (nb * nt) % CHUNK == 0 and CHUNK % nt == 0

    w_in, wo, wi, wo2 = _stage_weights(gla_w_in[0], gla_w_out[0], cmlp_w_in[0], cmlp_w_out[0])
    assert w_in.shape[1] == GLA_IN_PAD
    p = {
        "g0": norm_g[0].reshape(1, d),
        "g1": norm_g[1].reshape(1, d),
        "w_in": w_in,
        "wa": jnp.pad(gla_w_a_up[0], ((0, LANE - GLA_GATE_RANK), (0, 0))).astype(BF16),
        "ba": gla_b_a_up[0].reshape(1, GLA_KW),
        "gon": gla_g_onorm[0].reshape(1, GLA_DV),
        "wo": wo,
        "wi": wi,
        "lng": cmlp_ln_g[0].reshape(1, CMLP_WIDTH),
        "lnb": cmlp_ln_b[0].reshape(1, CMLP_WIDTH),
        "wo2": wo2,
        "gf": norm_final.reshape(1, d),
    }
    ws = cmlp_w_spatial[0]
    bs = cmlp_b_spatial[0]

    def spatial_params(c):
        reps = CHUNK // c
        ws_eff = jnp.tile(ws[:, :c, :c], (1, reps, reps))
        bs_full = jnp.repeat(jnp.tile(bs[:, :c], (1, reps)).T, CMLP_GROUP_DIM, axis=1)
        return ws_eff, bs_full

    y_prompt, s_prompt = _prompt_trunk(x_prompt, p, *spatial_params(CHUNK), PROMPT_CHUNKS_PER_STEP)

    xs = x_sample.reshape(nb * nt, d)
    q, k, v, gate, la = _gla_in(xs, p, 512)
    og, s_sample = _gla_sample(q, k, v, gate, la, p["gon"], state_gla[0], SAMPLE_SEQS_PER_STEP, nt)
    y_sample, v_rows = _out_cmlp(xs, og, p, *spatial_params(nt), nt, SAMPLE_CHUNKS_PER_STEP)

    return (y_prompt, y_sample.reshape(nb, nt, d), s_prompt[None], s_sample[None],
            v_rows.reshape(1, nb, nt, CMLP_WIDTH))
```

```python
import functools

import jax
import jax.numpy as jnp
from jax import lax
from jax.experimental import pallas as pl
from jax.experimental.pallas import tpu as pltpu

F32 = jnp.float32
BF16 = jnp.bfloat16

D_MODEL = 1024
GLA_HEADS = 4
GLA_DK = 128
GLA_DV = 256
GLA_KW = GLA_HEADS * GLA_DK
GLA_VW = GLA_HEADS * GLA_DV
GLA_GATE_RANK = 16
GLA_GATE_TAU = 16.0
CMLP_WIDTH = 1024
CMLP_GROUPS = 4
CMLP_GROUP_DIM = CMLP_WIDTH // CMLP_GROUPS
LANE = 128
CHUNK = 128
SUB = 8
PROMPT_CHUNKS_PER_STEP = 4
SAMPLE_CHUNKS_PER_STEP = 4
SAMPLE_SEQS_PER_STEP = 16
EPS = 1e-6
VMEM_LIMIT = 56 * 1024 * 1024
DECAY_GUARD = 60.0

_Q0, _K0, _V0, _G0, _A0 = 0, GLA_KW, 2 * GLA_KW, 2 * GLA_KW + GLA_VW, 2 * GLA_KW + 2 * GLA_VW
GLA_IN_PAD = _A0 + LANE


def _odd_tile_cols(cols):
    tiles = -(-cols // LANE)
    return (tiles + 1 - tiles % 2) * LANE


def _stage_weights_kernel(*refs):
    n = len(refs) // 2
    for src, dst in zip(refs[:n], refs[n:]):
        cols = src.shape[1]
        dst[:, 0:cols] = src[...].astype(BF16)
        if dst.shape[1] > cols:
            dst[:, cols:] = jnp.zeros((dst.shape[0], dst.shape[1] - cols), BF16)


def _stage_transposed_kernel(valid_rows, src, dst):
    i = pl.program_id(0)
    row = lax.broadcasted_iota(jnp.int32, src.shape, 0) + i * src.shape[0]
    dst[...] = jnp.where(row < valid_rows, src[...], 0.0).T.astype(BF16)


def _stage_transposed(w_t, cols):
    n, rows = w_t.shape
    assert cols % LANE == 0 and cols >= n
    return pl.pallas_call(
        functools.partial(_stage_transposed_kernel, n),
        grid=(cols // LANE,),
        in_specs=[pl.BlockSpec((LANE, rows), lambda i: (i, 0))],
        out_specs=pl.BlockSpec((rows, LANE), lambda i: (0, i)),
        out_shape=jax.ShapeDtypeStruct((rows, cols), BF16),
        compiler_params=pltpu.CompilerParams(dimension_semantics=("arbitrary",),
                                             vmem_limit_bytes=VMEM_LIMIT),
        name="stage_transposed",
    )(w_t)


def _stage_weights(*weights):
    rows = weights[0].shape[0]
    assert all(w.shape[0] == rows for w in weights)
    outs = [_odd_tile_cols(w.shape[1]) for w in weights]
    blk = lambda c: pl.BlockSpec((CHUNK, c), lambda i: (i, 0))
    return pl.pallas_call(
        _stage_weights_kernel,
        grid=(rows // CHUNK,),
        in_specs=[blk(w.shape[1]) for w in weights],
        out_specs=[blk(c) for c in outs],
        out_shape=[jax.ShapeDtypeStruct((rows, c), BF16) for c in outs],
        compiler_params=pltpu.CompilerParams(dimension_semantics=("arbitrary",),
                                             vmem_limit_bytes=VMEM_LIMIT),
        name="stage_weights",
    )(*weights)


def _dot(a, b):
    return jnp.dot(a, b, preferred_element_type=F32)


def _dot_nt(a, b):
    return lax.dot_general(a, b, (((1,), (1,)), ((), ())), preferred_element_type=F32)


def _rms(x, g):
    return x * lax.rsqrt(jnp.mean(x * x, axis=-1, keepdims=True) + EPS) * g


def _silu(x):
    return x * jax.nn.sigmoid(x)


def _neg_abs(x):
    return lax.bitcast_convert_type(
        lax.bitcast_convert_type(x, jnp.uint32) | jnp.uint32(0x80000000), F32)


def _split3(x):
    hi = x.astype(BF16)
    r1 = x - hi.astype(F32)
    mid = r1.astype(BF16)
    lo = (r1 - mid.astype(F32)).astype(BF16)
    return hi, mid, lo


def _masked_prefix_sum(mask01, x):
    hi, mid, lo = _split3(x)
    return (_dot(jnp.concatenate([mask01, mask01], axis=1), jnp.concatenate([hi, mid], axis=0))
            + _dot(mask01, lo))


def _sublane_bcast(x, row, group):
    r, n = x.shape
    x3 = x.reshape(r // group, group, n)
    return jnp.broadcast_to(x3[:, row:row + 1, :], x3.shape).reshape(r, n)


def _index_masks(r):
    row = lax.broadcasted_iota(jnp.int32, (r, r), 0)
    col = lax.broadcasted_iota(jnp.int32, (r, r), 1)
    return row ^ col, col <= row


def _log_decay(a_low, wa_ref, ba_ref):
    z = _dot(a_low.astype(BF16), wa_ref[...]) + ba_ref[...]
    log_sig = jnp.minimum(z, 0.0) - jnp.log1p(jnp.exp(-jnp.abs(z)))
    return log_sig * (1.0 / GLA_GATE_TAU)


def _layernorm(v, g, b):
    vc = v - jnp.mean(v, axis=-1, keepdims=True)
    var = jnp.mean(vc * vc, axis=-1, keepdims=True)
    return vc * lax.rsqrt(var + EPS) * g + b


def _intra_scores_factored(q, k, cum, levels, rxc, lower):
    r = q.shape[0]
    parts = []
    for s in range(SUB):
        kb = _sublane_bcast(k, s, SUB)
        cb = _sublane_bcast(cum, s, SUB)
        parts.append((q * kb * jnp.exp(jnp.minimum(cum - cb, 0.0))).astype(BF16))
    acat = jnp.concatenate(parts, axis=1)
    pr = lax.broadcasted_iota(jnp.int32, (SUB * GLA_DK, r), 0)
    pc = lax.broadcasted_iota(jnp.int32, (SUB * GLA_DK, r), 1)
    lane_sum = ((pc & (SUB - 1)) == (pr >> 7)).astype(BF16)
    scores = jnp.where((rxc < SUB) & lower, _dot(acat, lane_sum), 0.0)
    for m in levels:
        qk = jnp.concatenate([(q if (i // m) % 2 else k)[i:i + m] for i in range(0, r, m)], axis=0)
        ref = _sublane_bcast(cum, m - 1, 2 * m)
        f = (qk * jnp.exp(_neg_abs(cum - ref))).astype(BF16)
        scores = jnp.where((rxc >= m) & (rxc < 2 * m) & lower, _dot_nt(f, f), scores)
    return scores


def _intra_scores_direct(qin, k, cum, mask):
    kout = (k * jnp.exp(-cum)).astype(BF16)
    return jnp.where(mask, _dot_nt(qin, kout), 0.0)


def _out_norm_gate(o, gon, gate):
    return (_rms(o, gon) * gate).astype(BF16)


GLA_HEAD_STAGES = 3


def _gla_prompt_head(mild, h, rows, q_s, k_s, cum_s, v_s, gate_s, og_ref, s_in, s_out, gon_ref, rxc,
                     lower):
    ks = slice(h * GLA_DK, (h + 1) * GLA_DK)
    vs = slice(h * GLA_DV, (h + 1) * GLA_DV)
    q, k, cum = q_s[rows, ks], k_s[rows, ks], cum_s[rows, ks]
    vb = v_s[rows, vs]
    qin = (q * jnp.exp(cum)).astype(BF16)
    if mild:
        scores = _intra_scores_direct(qin, k, cum, lower)
    else:
        scores = _intra_scores_factored(q, k, cum, (64, 32, 16, 8), rxc, lower)
    k_t, cum_t = k.T, cum.T
    last = cum_t[:, CHUNK - 1:CHUNK]
    kd_t = (k_t * jnp.exp(last - cum_t)).astype(BF16)
    s_add = _dot(kd_t, vb)
    yield None
    state = s_in[h]
    o = _dot(jnp.concatenate([scores.astype(BF16), qin], axis=1),
             jnp.concatenate([vb, state.astype(BF16)], axis=0))
    s_out[h] = jnp.exp(last) * state + s_add
    yield None
    og_ref[rows, vs] = _out_norm_gate(o, gon_ref[...], gate_s[rows, vs])
    yield None


def _gla_prompt_chunk_stages(mild, c, *args):
    rows = slice(c * CHUNK, (c + 1) * CHUNK)
    gens = [_gla_prompt_head(mild, h, rows, *args) for h in range(GLA_HEADS)]
    for _ in range(GLA_HEAD_STAGES):
        for g in gens:
            next(g)
        yield None


def _alternate(first, second):
    last = None
    live_first = live_second = True
    while live_first or live_second:
        if live_first:
            live_first = next(first, StopIteration) is not StopIteration
        if live_second:
            item = next(second, StopIteration)
            live_second = item is not StopIteration
            last = item if live_second else last
    return last


def _cmlp_steps(x1, nchunks, mix_block, g1_ref, wi_ref, lng_ref, lnb_ref, ws_ref, bs_ref, wo2_ref,
                gf_ref, v_out_ref=None):
    w = CMLP_WIDTH
    hb = _rms(x1, g1_ref[...]).astype(BF16)
    yield None
    u = _dot(hb, wi_ref[:, 0:w])
    yield None
    vn = _layernorm(_dot(hb, wi_ref[:, w:2 * w]), lng_ref[...], lnb_ref[...])
    if v_out_ref is not None:
        v_out_ref[...] = vn
    yield None
    ug = u * _silu(_dot(hb, wi_ref[:, 2 * w:3 * w]))
    yield None
    vnb = vn.astype(BF16)
    rxc, lower = _index_masks(CHUNK)
    mix_mask = (rxc < mix_block) & lower
    zs = []
    for g in range(CMLP_GROUPS):
        gs = slice(g * CMLP_GROUP_DIM, (g + 1) * CMLP_GROUP_DIM)
        wsm = jnp.where(mix_mask, ws_ref[g], 0.0).astype(BF16)
        mixed = jnp.concatenate(
            [_dot(wsm, vnb[c * CHUNK:(c + 1) * CHUNK, gs]) + bs_ref[:, gs] for c in range(nchunks)],
            axis=0)
        zs.append((ug[:, gs] * mixed).astype(BF16))
    yield None
    acc = x1
    for g in range(CMLP_GROUPS):
        gs = slice(g * CMLP_GROUP_DIM, (g + 1) * CMLP_GROUP_DIM)
        acc = acc + _dot(zs[g], wo2_ref[gs, 0:D_MODEL])
        if g < CMLP_GROUPS - 1:
            yield None
    yield _rms(acc, gf_ref[...])


CMLP_STEPS = 9


def _drain(gen):
    out = None
    for out in gen:
        pass
    return out


def _prompt_kernel(nchunks, nblk, blk_per_seq, xa_ref, xc_ref, g0_ref, win_ref, wa_ref, ba_ref,
                   gon_ref, wo_ref, g1_ref, wi_ref, lng_ref, lnb_ref, ws_ref, bs_ref, wo2_ref, gf_ref,
                   y_ref, s_ref, q_s, k_s, cum_s, v_s, gate_s, og_s, st_s):
    j = pl.program_id(0)
    valid = j < nblk
    jj = jnp.minimum(j, nblk - 1)
    is_first = lax.rem(jj, blk_per_seq) == 0
    slot = lax.rem(j, 2)
    og_cur, og_prev = og_s.at[slot], og_s.at[1 - slot]

    @pl.when(j == 0)
    def _():
        og_s[...] = jnp.zeros_like(og_s)

    @pl.when(is_first)
    def _():
        s_ref[...] = jnp.zeros_like(s_ref)

    x1 = xc_ref[0] + _dot(og_prev[...], wo_ref[:, 0:D_MODEL])
    hb = _rms(xa_ref[0], g0_ref[...]).astype(BF16)
    la = _log_decay(_dot(hb, win_ref[:, _A0:GLA_IN_PAD]), wa_ref, ba_ref)
    q_s[...] = _dot(hb, win_ref[:, _Q0:_K0]) * (GLA_DK ** -0.5)
    k_s[...] = _dot(hb, win_ref[:, _K0:_V0])
    rxc, lower = _index_masks(CHUNK)
    tri = lower.astype(BF16)
    total = None
    for c in range(nchunks):
        rows = slice(c * CHUNK, (c + 1) * CHUNK)
        cum = _masked_prefix_sum(tri, la[rows])
        cum_s[rows, :] = cum
        end = jnp.min(cum[CHUNK - SUB:CHUNK, :])
        total = end if total is None else jnp.minimum(total, end)
    mild = total > -DECAY_GUARD
    cmlp = _cmlp_steps(x1, nchunks, CHUNK, g1_ref, wi_ref, lng_ref, lnb_ref, ws_ref, bs_ref, wo2_ref,
                       gf_ref)
    next(cmlp)
    v_s[...] = _dot(hb, win_ref[:, _V0:_G0]).astype(BF16)
    gate_s[...] = _silu(_dot(hb, win_ref[:, _G0:_A0]))

    def recurrence(is_mild, carried):
        for c in range(nchunks):
            yield from _gla_prompt_chunk_stages(
                is_mild, c, q_s, k_s, cum_s, v_s, gate_s, og_cur,
                s_ref.at[0] if c == 0 else carried, carried, gon_ref, rxc, lower)

    y_ref[0] = _alternate(recurrence(True, st_s), cmlp)
    commit = jnp.logical_and(valid, mild)
    for h in range(GLA_HEADS):
        s_ref[0, h] = jnp.where(commit, st_s[h], s_ref[0, h])

    @pl.when(jnp.logical_and(valid, jnp.logical_not(mild)))
    def _():
        _drain(recurrence(False, s_ref.at[0]))


def _const_spec(shape):
    nd = len(shape)
    return pl.BlockSpec(shape, lambda *_: (0,) * nd, pipeline_mode=pl.Buffered(1))


def _prompt_trunk(x, p, ws_eff, bs_full, nchunks):
    b, t, d = x.shape
    rb = nchunks * CHUNK
    bps = t // rb
    nblk = b * bps
    consts = [p["g0"], p["w_in"], p["wa"], p["ba"], p["gon"], p["wo"], p["g1"], p["wi"], p["lng"],
              p["lnb"], ws_eff, bs_full, p["wo2"], p["gf"]]

    def cur(j):
        jj = jnp.minimum(j, nblk - 1)
        return jj // bps, jj % bps

    def prev(j):
        jj = jnp.maximum(j - 1, 0)
        return jj // bps, jj % bps

    return pl.pallas_call(
        functools.partial(_prompt_kernel, nchunks, nblk, bps),
        grid=(nblk + 1,),
        in_specs=[pl.BlockSpec((1, rb, d), lambda j: (*cur(j), 0)),
                  pl.BlockSpec((1, rb, d), lambda j: (*prev(j), 0))]
        + [_const_spec(c.shape) for c in consts],
        out_specs=[pl.BlockSpec((1, rb, d), lambda j: (*prev(j), 0)),
                   pl.BlockSpec((1, GLA_HEADS, GLA_DK, GLA_DV), lambda j: (cur(j)[0], 0, 0, 0))],
        out_shape=[jax.ShapeDtypeStruct((b, t, d), F32),
                   jax.ShapeDtypeStruct((b, GLA_HEADS, GLA_DK, GLA_DV), F32)],
        scratch_shapes=[pltpu.VMEM((rb, GLA_KW), F32), pltpu.VMEM((rb, GLA_KW), F32),
                        pltpu.VMEM((rb, GLA_KW), F32), pltpu.VMEM((rb, GLA_VW), BF16),
                        pltpu.VMEM((rb, GLA_VW), F32), pltpu.VMEM((2, rb, GLA_VW), BF16),
                        pltpu.VMEM((GLA_HEADS, GLA_DK, GLA_DV), F32)],
        compiler_params=pltpu.CompilerParams(dimension_semantics=("arbitrary",),
                                             vmem_limit_bytes=VMEM_LIMIT),
        name="prompt_trunk",
    )(x, x, *consts)


def _gla_in_kernel(x_ref, g_ref, w_ref, wa_ref, ba_ref, q_ref, k_ref, v_ref, gate_ref, la_ref):
    hb = _rms(x_ref[...], g_ref[...]).astype(BF16)
    q_ref[...] = _dot(hb, w_ref[:, _Q0:_K0]) * (GLA_DK ** -0.5)
    k_ref[...] = _dot(hb, w_ref[:, _K0:_V0])
    v_ref[...] = _dot(hb, w_ref[:, _V0:_G0]).astype(BF16)
    gate_ref[...] = _silu(_dot(hb, w_ref[:, _G0:_A0]))
    la_ref[...] = _log_decay(_dot(hb, w_ref[:, _A0:GLA_IN_PAD]), wa_ref, ba_ref)


def _gla_in(x2d, p, tm):
    n = x2d.shape[0]
    row = lambda w: pl.BlockSpec((tm, w), lambda i: (i, 0))
    consts = [p["g0"], p["w_in"], p["wa"], p["ba"]]
    return pl.pallas_call(
        _gla_in_kernel,
        grid=(n // tm,),
        in_specs=[row(D_MODEL)] + [_const_spec(c.shape) for c in consts],
        out_specs=[row(GLA_KW), row(GLA_KW), row(GLA_VW), row(GLA_VW), row(GLA_KW)],
        out_shape=[jax.ShapeDtypeStruct((n, GLA_KW), F32), jax.ShapeDtypeStruct((n, GLA_KW), F32),
                   jax.ShapeDtypeStruct((n, GLA_VW), BF16), jax.ShapeDtypeStruct((n, GLA_VW), F32),
                   jax.ShapeDtypeStruct((n, GLA_KW), F32)],
        compiler_params=pltpu.CompilerParams(dimension_semantics=("arbitrary",),
                                             vmem_limit_bytes=VMEM_LIMIT),
        name="gla_in",
    )(x2d, *consts)


def _gla_sample_heads(mild, nseq, tlen, q_ref, k_ref, v_ref, gate_ref, cum_s, gon_ref, s_in_ref,
                      og_ref, s_out_ref, rxc, lower):
    r = nseq * tlen
    block_mask = (rxc < tlen) & lower
    seq_of_lane = lax.broadcasted_iota(jnp.int32, (GLA_DK, r), 1) >> 3
    for h in range(GLA_HEADS):
        ks = slice(h * GLA_DK, (h + 1) * GLA_DK)
        vs = slice(h * GLA_DV, (h + 1) * GLA_DV)
        q, k, cum = q_ref[:, ks], k_ref[:, ks], cum_s[:, ks]
        vb = v_ref[:, vs]
        qin = (q * jnp.exp(cum)).astype(BF16)
        if mild:
            scores = _intra_scores_direct(qin, k, cum, block_mask)
        else:
            scores = _intra_scores_factored(q, k, cum, (), rxc, lower)
        o_intra = _dot(scores.astype(BF16), vb)
        kd_t = (k * jnp.exp(_sublane_bcast(cum, tlen - 1, tlen) - cum)).T
        cum_t = cum.T
        for i in range(nseq):
            rs = slice(i * tlen, (i + 1) * tlen)
            state = s_in_ref[i, h]
            o = o_intra[rs] + _dot(qin[rs], state.astype(BF16))
            og_ref[rs, vs] = _out_norm_gate(o, gon_ref[...], gate_ref[rs, vs])
            last = cum_t[:, (i + 1) * tlen - 1:(i + 1) * tlen]
            kd_i = jnp.where(seq_of_lane == i, kd_t, 0.0).astype(BF16)
            s_out_ref[i, h] = jnp.exp(last) * state + _dot(kd_i, vb)


def _gla_sample_kernel(nseq, tlen, q_ref, k_ref, v_ref, gate_ref, la_ref, gon_ref, s_in_ref,
                       og_ref, s_out_ref, cum_s):
    rxc, lower = _index_masks(nseq * tlen)
    cum = _masked_prefix_sum(((rxc < tlen) & lower).astype(BF16), la_ref[...])
    cum_s[...] = cum
    mild = jnp.min(cum) > -DECAY_GUARD
    args = (nseq, tlen, q_ref, k_ref, v_ref, gate_ref, cum_s, gon_ref, s_in_ref, og_ref, s_out_ref,
            rxc, lower)
    pl.when(mild)(functools.partial(_gla_sample_heads, True, *args))
    pl.when(jnp.logical_not(mild))(functools.partial(_gla_sample_heads, False, *args))


def _gla_sample(q, k, v, gate, la, gon, state, nseq, tlen):
    n = q.shape[0]
    r = nseq * tlen
    assert tlen == SUB
    row = lambda w: pl.BlockSpec((r, w), lambda i: (i, 0))
    st = pl.BlockSpec((nseq, GLA_HEADS, GLA_DK, GLA_DV), lambda i: (i, 0, 0, 0))
    return pl.pallas_call(
        functools.partial(_gla_sample_kernel, nseq, tlen),
        grid=(n // r,),
        in_specs=[row(GLA_KW), row(GLA_KW), row(GLA_VW), row(GLA_VW), row(GLA_KW),
                  _const_spec((1, GLA_DV)), st],
        out_specs=[row(GLA_VW), st],
        out_shape=[jax.ShapeDtypeStruct((n, GLA_VW), BF16),
                   jax.ShapeDtypeStruct(state.shape, F32)],
        scratch_shapes=[pltpu.VMEM((r, GLA_KW), F32)],
        compiler_params=pltpu.CompilerParams(dimension_semantics=("arbitrary",),
                                             vmem_limit_bytes=VMEM_LIMIT),
        name="gla_sample",
    )(q, k, v, gate, la, gon, state)


def _out_cmlp_kernel(nchunks, mix_block, x_ref, og_ref, wo_ref, g1_ref, wi_ref, lng_ref, lnb_ref, ws_ref,
                     bs_ref, wo2_ref, gf_ref, y_ref, v_out_ref):
    x1 = x_ref[...] + _dot(og_ref[...], wo_ref[:, 0:D_MODEL])
    y_ref[...] = _drain(_cmlp_steps(x1, nchunks, mix_block, g1_ref, wi_ref, lng_ref, lnb_ref, ws_ref, bs_ref,
                                    wo2_ref, gf_ref, v_out_ref))


def _out_cmlp(x2d, og, p, ws_eff, bs_full, mix_block, nchunks):
    n = x2d.shape[0]
    rb = nchunks * CHUNK
    row = lambda w: pl.BlockSpec((rb, w), lambda i: (i, 0))
    consts = [p["wo"], p["g1"], p["wi"], p["lng"], p["lnb"], ws_eff, bs_full, p["wo2"], p["gf"]]
    return pl.pallas_call(
        functools.partial(_out_cmlp_kernel, nchunks, mix_block),
        grid=(n // rb,),
        in_specs=[row(D_MODEL), row(GLA_VW)] + [_const_spec(c.shape) for c in consts],
        out_specs=[row(D_MODEL), row(CMLP_WIDTH)],
        out_shape=[jax.ShapeDtypeStruct((n, D_MODEL), F32),
                   jax.ShapeDtypeStruct((n, CMLP_WIDTH), F32)],
        compiler_params=pltpu.CompilerParams(dimension_semantics=("arbitrary",),
                                             vmem_limit_bytes=VMEM_LIMIT),
        name="out_cmlp",
    )(x2d, og, *consts)


def kernel(x_prompt, x_sample, state_gla, norm_g, gla_w_in, gla_w_a_up, gla_b_a_up, gla_g_onorm,
           gla_w_out, cmlp_w_in, cmlp_ln_g, cmlp_ln_b, cmlp_w_spatial, cmlp_b_spatial, cmlp_w_out,
           norm_final):
    b, t, d = x_prompt.shape
    nb, nt, _ = x_sample.shape
    assert t % (PROMPT_CHUNKS_PER_STEP * CHUNK) == 0 and (nb * nt) % CHUNK == 0 and CHUNK % nt == 0

    w_in = _stage_transposed(gla_w_in[0].T, GLA_IN_PAD)
    wo, wi, wo2 = _stage_weights(gla_w_out[0], cmlp_w_in[0], cmlp_w_out[0])
    p = {
        "g0": norm_g[0].reshape(1, d),
        "g1": norm_g[1].reshape(1, d),
        "w_in": w_in,
        "wa": jnp.pad(gla_w_a_up[0], ((0, LANE - GLA_GATE_RANK), (0, 0))).astype(BF16),
        "ba": gla_b_a_up[0].reshape(1, GLA_KW),
        "gon": gla_g_onorm[0].reshape(1, GLA_DV),
        "wo": wo,
        "wi": wi,
        "lng": cmlp_ln_g[0].reshape(1, CMLP_WIDTH),
        "lnb": cmlp_ln_b[0].reshape(1, CMLP_WIDTH),
        "wo2": wo2,
        "gf": norm_final.reshape(1, d),
    }
    ws = cmlp_w_spatial[0]
    bs = cmlp_b_spatial[0]

    def spatial_params(c):
        reps = CHUNK // c
        ws_eff = jnp.tile(ws[:, :c, :c], (1, reps, reps))
        bs_full = jnp.repeat(jnp.tile(bs[:, :c], (1, reps)).T, CMLP_GROUP_DIM, axis=1)
        return ws_eff, bs_full

    y_prompt, s_prompt = _prompt_trunk(x_prompt, p, *spatial_params(CHUNK), PROMPT_CHUNKS_PER_STEP)

    xs = x_sample.reshape(nb * nt, d)
    q, k, v, gate, la = _gla_in(xs, p, 512)
    og, s_sample = _gla_sample(q, k, v, gate, la, p["gon"], state_gla[0], SAMPLE_SEQS_PER_STEP, nt)
    y_sample, v_rows = _out_cmlp(xs, og, p, *spatial_params(nt), nt, SAMPLE_CHUNKS_PER_STEP)

    return (y_prompt, y_sample.reshape(nb, nt, d), s_prompt[None], s_sample[None],
            v_rows.reshape(1, nb, nt, CMLP_WIDTH))
```

```python
import functools

import jax
import jax.numpy as jnp
from jax import lax
from jax.experimental import pallas as pl
from jax.experimental.pallas import tpu as pltpu

F32 = jnp.float32
BF16 = jnp.bfloat16

D_MODEL = 1024
GLA_HEADS = 4
GLA_DK = 128
GLA_DV = 256
GLA_KW = GLA_HEADS * GLA_DK
GLA_VW = GLA_HEADS * GLA_DV
GLA_GATE_RANK = 16
GLA_GATE_TAU = 16.0
CMLP_WIDTH = 1024
CMLP_GROUPS = 4
CMLP_GROUP_DIM = CMLP_WIDTH // CMLP_GROUPS
LANE = 128
CHUNK = 128
SUB = 8
PROMPT_CHUNKS_PER_STEP = 4
SAMPLE_CHUNKS_PER_STEP = 4
SAMPLE_SEQS_PER_STEP = 16
EPS = 1e-6
VMEM_LIMIT = 56 * 1024 * 1024
DECAY_GUARD = 60.0

_Q0, _K0, _V0, _G0, _A0 = 0, GLA_KW, 2 * GLA_KW, 2 * GLA_KW + GLA_VW, 2 * GLA_KW + 2 * GLA_VW
GLA_IN_PAD = _A0 + LANE


def _odd_tile_cols(cols):
    tiles = -(-cols // LANE)
    return (tiles + 1 - tiles % 2) * LANE


def _stage_weights_kernel(*refs):
    n = len(refs) // 2
    for src, dst in zip(refs[:n], refs[n:]):
        cols = src.shape[1]
        dst[:, 0:cols] = src[...].astype(BF16)
        if dst.shape[1] > cols:
            dst[:, cols:] = jnp.zeros((dst.shape[0], dst.shape[1] - cols), BF16)


def _stage_transposed_kernel(valid_rows, src, dst):
    i = pl.program_id(0)
    row = lax.broadcasted_iota(jnp.int32, src.shape, 0) + i * src.shape[0]
    dst[...] = jnp.where(row < valid_rows, src[...], 0.0).T.astype(BF16)


def _stage_transposed(w_t, cols):
    n, rows = w_t.shape
    blk = 5 * LANE
    assert cols % blk == 0 and cols >= n
    return pl.pallas_call(
        functools.partial(_stage_transposed_kernel, n),
        grid=(cols // blk,),
        in_specs=[pl.BlockSpec((blk, rows), lambda i: (i, 0))],
        out_specs=pl.BlockSpec((rows, blk), lambda i: (0, i)),
        out_shape=jax.ShapeDtypeStruct((rows, cols), BF16),
        compiler_params=pltpu.CompilerParams(dimension_semantics=("arbitrary",),
                                             vmem_limit_bytes=VMEM_LIMIT),
        name="stage_transposed",
    )(w_t)


def _stage_weights(*weights):
    rows = weights[0].shape[0]
    assert all(w.shape[0] == rows for w in weights)
    outs = [_odd_tile_cols(w.shape[1]) for w in weights]
    blk = lambda c: pl.BlockSpec((CHUNK, c), lambda i: (i, 0))
    return pl.pallas_call(
        _stage_weights_kernel,
        grid=(rows // CHUNK,),
        in_specs=[blk(w.shape[1]) for w in weights],
        out_specs=[blk(c) for c in outs],
        out_shape=[jax.ShapeDtypeStruct((rows, c), BF16) for c in outs],
        compiler_params=pltpu.CompilerParams(dimension_semantics=("arbitrary",),
                                             vmem_limit_bytes=VMEM_LIMIT),
        name="stage_weights",
    )(*weights)


def _dot(a, b):
    return jnp.dot(a, b, preferred_element_type=F32)


def _dot_nt(a, b):
    return lax.dot_general(a, b, (((1,), (1,)), ((), ())), preferred_element_type=F32)


def _rms(x, g):
    return x * lax.rsqrt(jnp.mean(x * x, axis=-1, keepdims=True) + EPS) * g


def _silu(x):
    return x * jax.nn.sigmoid(x)


def _masked_prefix_sum(mask01, x):
    hi = x.astype(BF16)
    mid = (x - hi.astype(F32)).astype(BF16)
    return _dot(jnp.concatenate([mask01, mask01], axis=1), jnp.concatenate([hi, mid], axis=0))


def _sublane_bcast(x, row, group):
    r, n = x.shape
    x3 = x.reshape(r // group, group, n)
    return jnp.broadcast_to(x3[:, row:row + 1, :], x3.shape).reshape(r, n)


def _index_masks(r):
    row = lax.broadcasted_iota(jnp.int32, (r, r), 0)
    col = lax.broadcasted_iota(jnp.int32, (r, r), 1)
    return row ^ col, col <= row


def _log_decay(a_low, wa_ref, ba_ref):
    z = _dot(a_low.astype(BF16), wa_ref[...]) + ba_ref[...]
    log_sig = jnp.minimum(z, 0.0) - jnp.log1p(jnp.exp(-jnp.abs(z)))
    return log_sig * (1.0 / GLA_GATE_TAU)


def _layernorm(v, g, b):
    vc = v - jnp.mean(v, axis=-1, keepdims=True)
    var = jnp.mean(vc * vc, axis=-1, keepdims=True)
    return vc * lax.rsqrt(var + EPS) * g + b


def _intra_scores_factored(q, k, cum, levels, rxc, lower):
    r = q.shape[0]
    parts = []
    for s in range(SUB):
        kb = _sublane_bcast(k, s, SUB)
        cb = _sublane_bcast(cum, s, SUB)
        parts.append((q * kb * jnp.exp(jnp.minimum(cum - cb, 0.0))).astype(BF16))
    acat = jnp.concatenate(parts, axis=1)
    pr = lax.broadcasted_iota(jnp.int32, (SUB * GLA_DK, r), 0)
    pc = lax.broadcasted_iota(jnp.int32, (SUB * GLA_DK, r), 1)
    lane_sum = ((pc & (SUB - 1)) == (pr >> 7)).astype(BF16)
    scores = jnp.where((rxc < SUB) & lower, _dot(acat, lane_sum), 0.0)
    for m in levels:
        qk = jnp.concatenate([(q if (i // m) % 2 else k)[i:i + m] for i in range(0, r, m)], axis=0)
        ref = _sublane_bcast(cum, m - 1, 2 * m)
        f = (qk * jnp.exp(-jnp.abs(cum - ref))).astype(BF16)
        scores = jnp.where((rxc >= m) & (rxc < 2 * m) & lower, _dot_nt(f, f), scores)
    return scores


def _intra_scores_direct(qin, k, cum, mask):
    kout = (k * jnp.exp(-cum)).astype(BF16)
    return jnp.where(mask, _dot_nt(qin, kout), 0.0)


def _out_norm_gate(o, gon, gate):
    return (_rms(o, gon) * gate).astype(BF16)


GLA_HEAD_STAGES = 3


def _gla_prompt_head(mild, h, rows, q_s, k_s, cum_s, v_s, gate_s, og_ref, s_in, s_out, gon_ref, rxc,
                     lower):
    ks = slice(h * GLA_DK, (h + 1) * GLA_DK)
    vs = slice(h * GLA_DV, (h + 1) * GLA_DV)
    q, k, cum = q_s[rows, ks], k_s[rows, ks], cum_s[rows, ks]
    vb = v_s[rows, vs]
    qin = (q * jnp.exp(cum)).astype(BF16)
    if mild:
        scores = _intra_scores_direct(qin, k, cum, lower)
    else:
        scores = _intra_scores_factored(q, k, cum, (64, 32, 16, 8), rxc, lower)
    k_t, cum_t = k.T, cum.T
    last = cum_t[:, CHUNK - 1:CHUNK]
    kd_t = (k_t * jnp.exp(last - cum_t)).astype(BF16)
    s_add = _dot(kd_t, vb)
    yield None
    state = s_in[h]
    o = _dot(jnp.concatenate([scores.astype(BF16), qin], axis=1),
             jnp.concatenate([vb, state.astype(BF16)], axis=0))
    s_out[h] = jnp.exp(last) * state + s_add
    yield None
    og_ref[rows, vs] = _out_norm_gate(o, gon_ref[...], gate_s[rows, vs])
    yield None


def _gla_prompt_chunk_stages(mild, c, *args):
    rows = slice(c * CHUNK, (c + 1) * CHUNK)
    gens = [_gla_prompt_head(mild, h, rows, *args) for h in range(GLA_HEADS)]
    for _ in range(GLA_HEAD_STAGES):
        for g in gens:
            next(g)
        yield None


def _alternate(first, second):
    last = None
    live_first = live_second = True
    while live_first or live_second:
        if live_first:
            live_first = next(first, StopIteration) is not StopIteration
        if live_second:
            item = next(second, StopIteration)
            live_second = item is not StopIteration
            last = item if live_second else last
    return last


def _cmlp_steps(x1, nchunks, mix_block, g1_ref, wi_ref, lng_ref, lnb_ref, ws_ref, bs_ref, wo2_ref,
                gf_ref, v_out_ref=None):
    w = CMLP_WIDTH
    hb = _rms(x1, g1_ref[...]).astype(BF16)
    yield None
    u = _dot(hb, wi_ref[:, 0:w])
    yield None
    vn = _layernorm(_dot(hb, wi_ref[:, w:2 * w]), lng_ref[...], lnb_ref[...])
    if v_out_ref is not None:
        v_out_ref[...] = vn
    yield None
    ug = u * _silu(_dot(hb, wi_ref[:, 2 * w:3 * w]))
    yield None
    vnb = vn.astype(BF16)
    rxc, lower = _index_masks(CHUNK)
    mix_mask = (rxc < mix_block) & lower
    zs = []
    for g in range(CMLP_GROUPS):
        gs = slice(g * CMLP_GROUP_DIM, (g + 1) * CMLP_GROUP_DIM)
        wsm = jnp.where(mix_mask, ws_ref[g], 0.0).astype(BF16)
        mixed = jnp.concatenate(
            [_dot(wsm, vnb[c * CHUNK:(c + 1) * CHUNK, gs]) + bs_ref[:, gs] for c in range(nchunks)],
            axis=0)
        zs.append((ug[:, gs] * mixed).astype(BF16))
    yield None
    acc = x1
    for g in range(CMLP_GROUPS):
        gs = slice(g * CMLP_GROUP_DIM, (g + 1) * CMLP_GROUP_DIM)
        acc = acc + _dot(zs[g], wo2_ref[gs, 0:D_MODEL])
        if g < CMLP_GROUPS - 1:
            yield None
    yield _rms(acc, gf_ref[...])


CMLP_STEPS = 9


def _drain(gen):
    out = None
    for out in gen:
        pass
    return out


def _prompt_kernel(nchunks, nblk, blk_per_seq, xa_ref, xc_ref, g0_ref, win_ref, wa_ref, ba_ref,
                   gon_ref, wo_ref, g1_ref, wi_ref, lng_ref, lnb_ref, ws_ref, bs_ref, wo2_ref, gf_ref,
                   y_ref, s_ref, q_s, k_s, cum_s, v_s, gate_s, og_s, st_s):
    j = pl.program_id(0)
    valid = j < nblk
    jj = jnp.minimum(j, nblk - 1)
    is_first = lax.rem(jj, blk_per_seq) == 0
    slot = lax.rem(j, 2)
    og_cur, og_prev = og_s.at[slot], og_s.at[1 - slot]

    @pl.when(j == 0)
    def _():
        og_s[...] = jnp.zeros_like(og_s)

    @pl.when(is_first)
    def _():
        s_ref[...] = jnp.zeros_like(s_ref)

    x1 = xc_ref[0] + _dot(og_prev[...], wo_ref[:, 0:D_MODEL])
    hb = _rms(xa_ref[0], g0_ref[...]).astype(BF16)
    la = _log_decay(_dot(hb, win_ref[:, _A0:GLA_IN_PAD]), wa_ref, ba_ref)
    q_s[...] = _dot(hb, win_ref[:, _Q0:_K0]) * (GLA_DK ** -0.5)
    k_s[...] = _dot(hb, win_ref[:, _K0:_V0])
    rxc, lower = _index_masks(CHUNK)
    tri = lower.astype(BF16)
    total = None
    for c in range(nchunks):
        rows = slice(c * CHUNK, (c + 1) * CHUNK)
        cum = _masked_prefix_sum(tri, la[rows])
        cum_s[rows, :] = cum
        end = jnp.min(cum[CHUNK - SUB:CHUNK, :])
        total = end if total is None else jnp.minimum(total, end)
    mild = total > -DECAY_GUARD
    cmlp = _cmlp_steps(x1, nchunks, CHUNK, g1_ref, wi_ref, lng_ref, lnb_ref, ws_ref, bs_ref, wo2_ref,
                       gf_ref)
    next(cmlp)
    v_s[...] = _dot(hb, win_ref[:, _V0:_G0]).astype(BF16)
    gate_s[...] = _silu(_dot(hb, win_ref[:, _G0:_A0]))

    def recurrence(is_mild, carried):
        for c in range(nchunks):
            yield from _gla_prompt_chunk_stages(
                is_mild, c, q_s, k_s, cum_s, v_s, gate_s, og_cur,
                s_ref.at[0] if c == 0 else carried, carried, gon_ref, rxc, lower)

    y_ref[0] = _alternate(recurrence(True, st_s), cmlp)
    commit = jnp.logical_and(valid, mild)
    for h in range(GLA_HEADS):
        s_ref[0, h] = jnp.where(commit, st_s[h], s_ref[0, h])

    @pl.when(jnp.logical_and(valid, jnp.logical_not(mild)))
    def _():
        _drain(recurrence(False, s_ref.at[0]))


def _const_spec(shape):
    nd = len(shape)
    return pl.BlockSpec(shape, lambda *_: (0,) * nd, pipeline_mode=pl.Buffered(1))


def _prompt_trunk(x, p, ws_eff, bs_full, nchunks):
    b, t, d = x.shape
    rb = nchunks * CHUNK
    bps = t // rb
    nblk = b * bps
    consts = [p["g0"], p["w_in"], p["wa"], p["ba"], p["gon"], p["wo"], p["g1"], p["wi"], p["lng"],
              p["lnb"], ws_eff, bs_full, p["wo2"], p["gf"]]

    def cur(j):
        jj = jnp.minimum(j, nblk - 1)
        return jj // bps, jj % bps

    def prev(j):
        jj = jnp.maximum(j - 1, 0)
        return jj // bps, jj % bps

    return pl.pallas_call(
        functools.partial(_prompt_kernel, nchunks, nblk, bps),
        grid=(nblk + 1,),
        in_specs=[pl.BlockSpec((1, rb, d), lambda j: (*cur(j), 0)),
                  pl.BlockSpec((1, rb, d), lambda j: (*prev(j), 0))]
        + [_const_spec(c.shape) for c in consts],
        out_specs=[pl.BlockSpec((1, rb, d), lambda j: (*prev(j), 0)),
                   pl.BlockSpec((1, GLA_HEADS, GLA_DK, GLA_DV), lambda j: (cur(j)[0], 0, 0, 0))],
        out_shape=[jax.ShapeDtypeStruct((b, t, d), F32),
                   jax.ShapeDtypeStruct((b, GLA_HEADS, GLA_DK, GLA_DV), F32)],
        scratch_shapes=[pltpu.VMEM((rb, GLA_KW), F32), pltpu.VMEM((rb, GLA_KW), F32),
                        pltpu.VMEM((rb, GLA_KW), F32), pltpu.VMEM((rb, GLA_VW), BF16),
                        pltpu.VMEM((rb, GLA_VW), F32), pltpu.VMEM((2, rb, GLA_VW), BF16),
                        pltpu.VMEM((GLA_HEADS, GLA_DK, GLA_DV), F32)],
        compiler_params=pltpu.CompilerParams(dimension_semantics=("arbitrary",),
                                             vmem_limit_bytes=VMEM_LIMIT),
        name="prompt_trunk",
    )(x, x, *consts)


def _gla_in_kernel(x_ref, g_ref, w_ref, wa_ref, ba_ref, q_ref, k_ref, v_ref, gate_ref, la_ref):
    hb = _rms(x_ref[...], g_ref[...]).astype(BF16)
    q_ref[...] = _dot(hb, w_ref[:, _Q0:_K0]) * (GLA_DK ** -0.5)
    k_ref[...] = _dot(hb, w_ref[:, _K0:_V0])
    v_ref[...] = _dot(hb, w_ref[:, _V0:_G0]).astype(BF16)
    gate_ref[...] = _silu(_dot(hb, w_ref[:, _G0:_A0]))
    la_ref[...] = _log_decay(_dot(hb, w_ref[:, _A0:GLA_IN_PAD]), wa_ref, ba_ref)


def _gla_in(x2d, p, tm):
    n = x2d.shape[0]
    row = lambda w: pl.BlockSpec((tm, w), lambda i: (i, 0))
    consts = [p["g0"], p["w_in"], p["wa"], p["ba"]]
    return pl.pallas_call(
        _gla_in_kernel,
        grid=(n // tm,),
        in_specs=[row(D_MODEL)] + [_const_spec(c.shape) for c in consts],
        out_specs=[row(GLA_KW), row(GLA_KW), row(GLA_VW), row(GLA_VW), row(GLA_KW)],
        out_shape=[jax.ShapeDtypeStruct((n, GLA_KW), F32), jax.ShapeDtypeStruct((n, GLA_KW), F32),
                   jax.ShapeDtypeStruct((n, GLA_VW), BF16), jax.ShapeDtypeStruct((n, GLA_VW), F32),
                   jax.ShapeDtypeStruct((n, GLA_KW), F32)],
        compiler_params=pltpu.CompilerParams(dimension_semantics=("arbitrary",),
                                             vmem_limit_bytes=VMEM_LIMIT),
        name="gla_in",
    )(x2d, *consts)


def _gla_sample_heads(mild, nseq, tlen, q_ref, k_ref, v_ref, gate_ref, cum_s, gon_ref, s_in_ref,
                      og_ref, s_out_ref, rxc, lower):
    r = nseq * tlen
    block_mask = (rxc < tlen) & lower
    seq_of_lane = lax.broadcasted_iota(jnp.int32, (GLA_DK, r), 1) >> 3
    for h in range(GLA_HEADS):
        ks = slice(h * GLA_DK, (h + 1) * GLA_DK)
        vs = slice(h * GLA_DV, (h + 1) * GLA_DV)
        q, k, cum = q_ref[:, ks], k_ref[:, ks], cum_s[:, ks]
        vb = v_ref[:, vs]
        qin = (q * jnp.exp(cum)).astype(BF16)
        if mild:
            scores = _intra_scores_direct(qin, k, cum, block_mask)
        else:
            scores = _intra_scores_factored(q, k, cum, (), rxc, lower)
        o_intra = _dot(scores.astype(BF16), vb)
        kd_t = (k * jnp.exp(_sublane_bcast(cum, tlen - 1, tlen) - cum)).T
        cum_t = cum.T
        for i in range(nseq):
            rs = slice(i * tlen, (i + 1) * tlen)
            state = s_in_ref[i, h]
            o = o_intra[rs] + _dot(qin[rs], state.astype(BF16))
            og_ref[rs, vs] = _out_norm_gate(o, gon_ref[...], gate_ref[rs, vs])
            last = cum_t[:, (i + 1) * tlen - 1:(i + 1) * tlen]
            kd_i = jnp.where(seq_of_lane == i, kd_t, 0.0).astype(BF16)
            s_out_ref[i, h] = jnp.exp(last) * state + _dot(kd_i, vb)


def _gla_sample_kernel(nseq, tlen, q_ref, k_ref, v_ref, gate_ref, la_ref, gon_ref, s_in_ref,
                       og_ref, s_out_ref, cum_s):
    rxc, lower = _index_masks(nseq * tlen)
    cum = _masked_prefix_sum(((rxc < tlen) & lower).astype(BF16), la_ref[...])
    cum_s[...] = cum
    mild = jnp.min(cum) > -DECAY_GUARD
    args = (nseq, tlen, q_ref, k_ref, v_ref, gate_ref, cum_s, gon_ref, s_in_ref, og_ref, s_out_ref,
            rxc, lower)
    pl.when(mild)(functools.partial(_gla_sample_heads, True, *args))
    pl.when(jnp.logical_not(mild))(functools.partial(_gla_sample_heads, False, *args))


def _gla_sample(q, k, v, gate, la, gon, state, nseq, tlen):
    n = q.shape[0]
    r = nseq * tlen
    assert tlen == SUB
    row = lambda w: pl.BlockSpec((r, w), lambda i: (i, 0))
    st = pl.BlockSpec((nseq, GLA_HEADS, GLA_DK, GLA_DV), lambda i: (i, 0, 0, 0))
    return pl.pallas_call(
        functools.partial(_gla_sample_kernel, nseq, tlen),
        grid=(n // r,),
        in_specs=[row(GLA_KW), row(GLA_KW), row(GLA_VW), row(GLA_VW), row(GLA_KW),
                  _const_spec((1, GLA_DV)), st],
        out_specs=[row(GLA_VW), st],
        out_shape=[jax.ShapeDtypeStruct((n, GLA_VW), BF16),
                   jax.ShapeDtypeStruct(state.shape, F32)],
        scratch_shapes=[pltpu.VMEM((r, GLA_KW), F32)],
        compiler_params=pltpu.CompilerParams(dimension_semantics=("arbitrary",),
                                             vmem_limit_bytes=VMEM_LIMIT),
        name="gla_sample",
    )(q, k, v, gate, la, gon, state)


def _out_cmlp_kernel(nchunks, mix_block, x_ref, og_ref, wo_ref, g1_ref, wi_ref, lng_ref, lnb_ref, ws_ref,
                     bs_ref, wo2_ref, gf_ref, y_ref, v_out_ref):
    x1 = x_ref[...] + _dot(og_ref[...], wo_ref[:, 0:D_MODEL])
    y_ref[...] = _drain(_cmlp_steps(x1, nchunks, mix_block, g1_ref, wi_ref, lng_ref, lnb_ref, ws_ref, bs_ref,
                                    wo2_ref, gf_ref, v_out_ref))


def _out_cmlp(x2d, og, p, ws_eff, bs_full, mix_block, nchunks):
    n = x2d.shape[0]
    rb = nchunks * CHUNK
    row = lambda w: pl.BlockSpec((rb, w), lambda i: (i, 0))
    consts = [p["wo"], p["g1"], p["wi"], p["lng"], p["lnb"], ws_eff, bs_full, p["wo2"], p["gf"]]
    return pl.pallas_call(
        functools.partial(_out_cmlp_kernel, nchunks, mix_block),
        grid=(n // rb,),
        in_specs=[row(D_MODEL), row(GLA_VW)] + [_const_spec(c.shape) for c in consts],
        out_specs=[row(D_MODEL), row(CMLP_WIDTH)],
        out_shape=[jax.ShapeDtypeStruct((n, D_MODEL), F32),
                   jax.ShapeDtypeStruct((n, CMLP_WIDTH), F32)],
        compiler_params=pltpu.CompilerParams(dimension_semantics=("arbitrary",),
                                             vmem_limit_bytes=VMEM_LIMIT),
        name="out_cmlp",
    )(x2d, og, *consts)


def kernel(x_prompt, x_sample, state_gla, norm_g, gla_w_in, gla_w_a_up, gla_b_a_up, gla_g_onorm,
           gla_w_out, cmlp_w_in, cmlp_ln_g, cmlp_ln_b, cmlp_w_spatial, cmlp_b_spatial, cmlp_w_out,
           norm_final):
    b, t, d = x_prompt.shape
    nb, nt, _ = x_sample.shape
    assert t % (PROMPT_CHUNKS_PER_STEP * CHUNK) == 0 and (nb * nt) % CHUNK == 0 and CHUNK % nt == 0

    w_in = _stage_transposed(gla_w_in[0].T, GLA_IN_PAD)
    wo, wi, wo2 = _stage_weights(gla_w_out[0], cmlp_w_in[0], cmlp_w_out[0])
    p = {
        "g0": norm_g[0].reshape(1, d),
        "g1": norm_g[1].reshape(1, d),
        "w_in": w_in,
        "wa": jnp.pad(gla_w_a_up[0], ((0, LANE - GLA_GATE_RANK), (0, 0))).astype(BF16),
        "ba": gla_b_a_up[0].reshape(1, GLA_KW),
        "gon": gla_g_onorm[0].reshape(1, GLA_DV),
        "wo": wo,
        "wi": wi,
        "lng": cmlp_ln_g[0].reshape(1, CMLP_WIDTH),
        "lnb": cmlp_ln_b[0].reshape(1, CMLP_WIDTH),
        "wo2": wo2,
        "gf": norm_final.reshape(1, d),
    }
    ws = cmlp_w_spatial[0]
    bs = cmlp_b_spatial[0]

    def spatial_params(c):
        reps = CHUNK // c
        ws_eff = jnp.tile(ws[:, :c, :c], (1, reps, reps))
        bs_full = jnp.repeat(jnp.tile(bs[:, :c], (1, reps)).T, CMLP_GROUP_DIM, axis=1)
        return ws_eff, bs_full

    y_prompt, s_prompt = _prompt_trunk(x_prompt, p, *spatial_params(CHUNK), PROMPT_CHUNKS_PER_STEP)

    xs = x_sample.reshape(nb * nt, d)
    q, k, v, gate, la = _gla_in(xs, p, 512)
    og, s_sample = _gla_sample(q, k, v, gate, la, p["gon"], state_gla[0], SAMPLE_SEQS_PER_STEP, nt)
    y_sample, v_rows = _out_cmlp(xs, og, p, *spatial_params(nt), nt, SAMPLE_CHUNKS_PER_STEP)

    return (y_prompt, y_sample.reshape(nb, nt, d), s_prompt[None], s_sample[None],
            v_rows.reshape(1, nb, nt, CMLP_WIDTH))
```

```python
import functools

import jax
import jax.numpy as jnp
from jax import lax
from jax.experimental import pallas as pl
from jax.experimental.pallas import tpu as pltpu

F32 = jnp.float32
BF16 = jnp.bfloat16

D_MODEL = 1024
GLA_HEADS = 4
GLA_DK = 128
GLA_DV = 256
GLA_KW = GLA_HEADS * GLA_DK
GLA_VW = GLA_HEADS * GLA_DV
GLA_GATE_RANK = 16
GLA_GATE_TAU = 16.0
CMLP_WIDTH = 1024
CMLP_GROUPS = 4
CMLP_GROUP_DIM = CMLP_WIDTH // CMLP_GROUPS
LANE = 128
CHUNK = 128
SUB = 8
PROMPT_CHUNKS_PER_STEP = 4
SAMPLE_CHUNKS_PER_STEP = 4
SAMPLE_SEQS_PER_STEP = 16
EPS = 1e-6
VMEM_LIMIT = 56 * 1024 * 1024
DECAY_GUARD = 60.0

_Q0, _K0, _V0, _G0, _A0 = 0, GLA_KW, 2 * GLA_KW, 2 * GLA_KW + GLA_VW, 2 * GLA_KW + 2 * GLA_VW
GLA_IN_PAD = _A0 + LANE


def _odd_tile_cols(cols):
    tiles = -(-cols // LANE)
    return (tiles + 1 - tiles % 2) * LANE


def _stage_weights_kernel(*refs):
    n = len(refs) // 2
    for src, dst in zip(refs[:n], refs[n:]):
        cols = src.shape[1]
        dst[:, 0:cols] = src[...].astype(BF16)
        if dst.shape[1] > cols:
            dst[:, cols:] = jnp.zeros((dst.shape[0], dst.shape[1] - cols), BF16)


def _stage_transposed_kernel(valid_rows, src, dst):
    i = pl.program_id(0)
    row = lax.broadcasted_iota(jnp.int32, src.shape, 0) + i * src.shape[0]
    dst[...] = jnp.where(row < valid_rows, src[...], 0.0).T.astype(BF16)


def _stage_transposed(w_t, cols):
    n, rows = w_t.shape
    blk = 5 * LANE
    assert cols % blk == 0 and cols >= n
    return pl.pallas_call(
        functools.partial(_stage_transposed_kernel, n),
        grid=(cols // blk,),
        in_specs=[pl.BlockSpec((blk, rows), lambda i: (i, 0))],
        out_specs=pl.BlockSpec((rows, blk), lambda i: (0, i)),
        out_shape=jax.ShapeDtypeStruct((rows, cols), BF16),
        compiler_params=pltpu.CompilerParams(dimension_semantics=("arbitrary",),
                                             vmem_limit_bytes=VMEM_LIMIT),
        name="stage_transposed",
    )(w_t)


def _stage_weights(*weights):
    rows = weights[0].shape[0]
    assert all(w.shape[0] == rows for w in weights)
    outs = [_odd_tile_cols(w.shape[1]) for w in weights]
    blk = lambda c: pl.BlockSpec((CHUNK, c), lambda i: (i, 0))
    return pl.pallas_call(
        _stage_weights_kernel,
        grid=(rows // CHUNK,),
        in_specs=[blk(w.shape[1]) for w in weights],
        out_specs=[blk(c) for c in outs],
        out_shape=[jax.ShapeDtypeStruct((rows, c), BF16) for c in outs],
        compiler_params=pltpu.CompilerParams(dimension_semantics=("arbitrary",),
                                             vmem_limit_bytes=VMEM_LIMIT),
        name="stage_weights",
    )(*weights)


def _dot(a, b):
    return jnp.dot(a, b, preferred_element_type=F32)


def _dot_nt(a, b):
    return lax.dot_general(a, b, (((1,), (1,)), ((), ())), preferred_element_type=F32)


def _rms(x, g):
    return x * lax.rsqrt(jnp.mean(x * x, axis=-1, keepdims=True) + EPS) * g


def _silu(x):
    return x * jax.nn.sigmoid(x)


def _masked_prefix_sum(mask01, x):
    hi = x.astype(BF16)
    mid = (x - hi.astype(F32)).astype(BF16)
    return _dot(jnp.concatenate([mask01, mask01], axis=1), jnp.concatenate([hi, mid], axis=0))


def _sublane_bcast(x, row, group):
    r, n = x.shape
    x3 = x.reshape(r // group, group, n)
    return jnp.broadcast_to(x3[:, row:row + 1, :], x3.shape).reshape(r, n)


def _index_masks(r):
    row = lax.broadcasted_iota(jnp.int32, (r, r), 0)
    col = lax.broadcasted_iota(jnp.int32, (r, r), 1)
    return row ^ col, col <= row


def _log_decay(a_low, wa_ref, ba_ref):
    z = _dot(a_low.astype(BF16), wa_ref[...]) + ba_ref[...]
    log_sig = jnp.minimum(z, 0.0) - jnp.log1p(jnp.exp(-jnp.abs(z)))
    return log_sig * (1.0 / GLA_GATE_TAU)


def _layernorm(v, g, b):
    vc = v - jnp.mean(v, axis=-1, keepdims=True)
    var = jnp.mean(vc * vc, axis=-1, keepdims=True)
    return vc * lax.rsqrt(var + EPS) * g + b


def _intra_scores_factored(q, k, cum, levels, rxc, lower):
    r = q.shape[0]
    parts = []
    for s in range(SUB):
        kb = _sublane_bcast(k, s, SUB)
        cb = _sublane_bcast(cum, s, SUB)
        parts.append((q * kb * jnp.exp(jnp.minimum(cum - cb, 0.0))).astype(BF16))
    acat = jnp.concatenate(parts, axis=1)
    pr = lax.broadcasted_iota(jnp.int32, (SUB * GLA_DK, r), 0)
    pc = lax.broadcasted_iota(jnp.int32, (SUB * GLA_DK, r), 1)
    lane_sum = ((pc & (SUB - 1)) == (pr >> 7)).astype(BF16)
    scores = jnp.where((rxc < SUB) & lower, _dot(acat, lane_sum), 0.0)
    for m in levels:
        qk = jnp.concatenate([(q if (i // m) % 2 else k)[i:i + m] for i in range(0, r, m)], axis=0)
        ref = _sublane_bcast(cum, m - 1, 2 * m)
        f = (qk * jnp.exp(-jnp.abs(cum - ref))).astype(BF16)
        scores = jnp.where((rxc >= m) & (rxc < 2 * m) & lower, _dot_nt(f, f), scores)
    return scores


def _intra_scores_direct(qin, k, cum, mask):
    kout = (k * jnp.exp(-cum)).astype(BF16)
    return jnp.where(mask, _dot_nt(qin, kout), 0.0)


def _out_norm_gate(o, gon, gate):
    return (_rms(o, gon) * gate).astype(BF16)


GLA_HEAD_STAGES = 3


def _gla_prompt_head(mild, h, rows, q_s, k_s, cum_s, v_s, gate_s, og_ref, s_in, s_out, gon_ref, rxc,
                     lower):
    ks = slice(h * GLA_DK, (h + 1) * GLA_DK)
    vs = slice(h * GLA_DV, (h + 1) * GLA_DV)
    q, k, cum = q_s[rows, ks], k_s[rows, ks], cum_s[rows, ks]
    vb = v_s[rows, vs]
    qin = (q * jnp.exp(cum)).astype(BF16)
    if mild:
        scores = _intra_scores_direct(qin, k, cum, lower)
    else:
        scores = _intra_scores_factored(q, k, cum, (64, 32, 16, 8), rxc, lower)
    k_t, cum_t = k.T, cum.T
    last = cum_t[:, CHUNK - 1:CHUNK]
    kd_t = (k_t * jnp.exp(last - cum_t)).astype(BF16)
    s_add = _dot(kd_t, vb)
    yield None
    state = s_in[h]
    o = _dot(jnp.concatenate([scores.astype(BF16), qin], axis=1),
             jnp.concatenate([vb, state.astype(BF16)], axis=0))
    s_out[h] = jnp.exp(last) * state + s_add
    yield None
    og_ref[rows, vs] = _out_norm_gate(o, gon_ref[...], gate_s[rows, vs])
    yield None


def _gla_prompt_chunk_stages(mild, c, *args):
    rows = slice(c * CHUNK, (c + 1) * CHUNK)
    gens = [_gla_prompt_head(mild, h, rows, *args) for h in range(GLA_HEADS)]
    for _ in range(GLA_HEAD_STAGES):
        for g in gens:
            next(g)
        yield None


def _alternate(first, second):
    last = None
    live_first = live_second = True
    while live_first or live_second:
        if live_first:
            live_first = next(first, StopIteration) is not StopIteration
        if live_second:
            item = next(second, StopIteration)
            live_second = item is not StopIteration
            last = item if live_second else last
    return last


def _cmlp_steps(x1, nchunks, mix_block, g1_ref, wi_ref, lng_ref, lnb_ref, ws_ref, bs_ref, wo2_ref,
                gf_ref, v_out_ref=None):
    w = CMLP_WIDTH
    hb = _rms(x1, g1_ref[...]).astype(BF16)
    yield None
    u = _dot(hb, wi_ref[:, 0:w])
    yield None
    vn = _layernorm(_dot(hb, wi_ref[:, w:2 * w]), lng_ref[...], lnb_ref[...])
    if v_out_ref is not None:
        v_out_ref[...] = vn
    yield None
    ug = u * _silu(_dot(hb, wi_ref[:, 2 * w:3 * w]))
    yield None
    vnb = vn.astype(BF16)
    rxc, lower = _index_masks(CHUNK)
    mix_mask = (rxc < mix_block) & lower
    zs = []
    for g in range(CMLP_GROUPS):
        gs = slice(g * CMLP_GROUP_DIM, (g + 1) * CMLP_GROUP_DIM)
        wsm = jnp.where(mix_mask, ws_ref[g], 0.0).astype(BF16)
        mixed = jnp.concatenate(
            [_dot(wsm, vnb[c * CHUNK:(c + 1) * CHUNK, gs]) + bs_ref[:, gs] for c in range(nchunks)],
            axis=0)
        zs.append((ug[:, gs] * mixed).astype(BF16))
    yield None
    acc = x1
    for g in range(CMLP_GROUPS):
        gs = slice(g * CMLP_GROUP_DIM, (g + 1) * CMLP_GROUP_DIM)
        acc = acc + _dot(zs[g], wo2_ref[gs, 0:D_MODEL])
        if g < CMLP_GROUPS - 1:
            yield None
    yield _rms(acc, gf_ref[...])


CMLP_STEPS = 9


def _drain(gen):
    out = None
    for out in gen:
        pass
    return out


def _prompt_kernel(nchunks, nblk, blk_per_seq, xa_ref, xc_ref, g0_ref, win_ref, wa_ref, ba_ref,
                   gon_ref, wo_ref, g1_ref, wi_ref, lng_ref, lnb_ref, ws_ref, bs_ref, wo2_ref, gf_ref,
                   y_ref, s_ref, q_s, k_s, cum_s, v_s, gate_s, og_s, st_s):
    j = pl.program_id(0)
    valid = j < nblk
    jj = jnp.minimum(j, nblk - 1)
    is_first = lax.rem(jj, blk_per_seq) == 0
    slot = lax.rem(j, 2)
    og_cur, og_prev = og_s.at[slot], og_s.at[1 - slot]

    @pl.when(j == 0)
    def _():
        og_s[...] = jnp.zeros_like(og_s)

    @pl.when(is_first)
    def _():
        s_ref[...] = jnp.zeros_like(s_ref)

    x1 = xc_ref[0] + _dot(og_prev[...], wo_ref[:, 0:D_MODEL])
    hb = _rms(xa_ref[0], g0_ref[...]).astype(BF16)
    la = _log_decay(_dot(hb, win_ref[:, _A0:GLA_IN_PAD]), wa_ref, ba_ref)
    q_s[...] = _dot(hb, win_ref[:, _Q0:_K0]) * (GLA_DK ** -0.5)
    k_s[...] = _dot(hb, win_ref[:, _K0:_V0])
    rxc, lower = _index_masks(CHUNK)
    tri = lower.astype(BF16)
    total = None
    for c in range(nchunks):
        rows = slice(c * CHUNK, (c + 1) * CHUNK)
        cum = _masked_prefix_sum(tri, la[rows])
        cum_s[rows, :] = cum
        end = jnp.min(cum[CHUNK - SUB:CHUNK, :])
        total = end if total is None else jnp.minimum(total, end)
    mild = total > -DECAY_GUARD
    cmlp = _cmlp_steps(x1, nchunks, CHUNK, g1_ref, wi_ref, lng_ref, lnb_ref, ws_ref, bs_ref, wo2_ref,
                       gf_ref)
    next(cmlp)
    v_s[...] = _dot(hb, win_ref[:, _V0:_G0]).astype(BF16)
    gate_s[...] = _silu(_dot(hb, win_ref[:, _G0:_A0]))

    def recurrence(is_mild, carried):
        for c in range(nchunks):
            yield from _gla_prompt_chunk_stages(
                is_mild, c, q_s, k_s, cum_s, v_s, gate_s, og_cur,
                s_ref.at[0] if c == 0 else carried, carried, gon_ref, rxc, lower)

    y_ref[0] = _alternate(recurrence(True, st_s), cmlp)
    commit = jnp.logical_and(valid, mild)
    for h in range(GLA_HEADS):
        s_ref[0, h] = jnp.where(commit, st_s[h], s_ref[0, h])

    @pl.when(jnp.logical_and(valid, jnp.logical_not(mild)))
    def _():
        _drain(recurrence(False, s_ref.at[0]))


def _const_spec(shape):
    nd = len(shape)
    return pl.BlockSpec(shape, lambda *_: (0,) * nd, pipeline_mode=pl.Buffered(1))


def _prompt_trunk(x, p, ws_eff, bs_full, nchunks):
    b, t, d = x.shape
    rb = nchunks * CHUNK
    bps = t // rb
    nblk = b * bps
    consts = [p["g0"], p["w_in"], p["wa"], p["ba"], p["gon"], p["wo"], p["g1"], p["wi"], p["lng"],
              p["lnb"], ws_eff, bs_full, p["wo2"], p["gf"]]

    def cur(j):
        jj = jnp.minimum(j, nblk - 1)
        return jj // bps, jj % bps

    def prev(j):
        jj = jnp.maximum(j - 1, 0)
        return jj // bps, jj % bps

    return pl.pallas_call(
        functools.partial(_prompt_kernel, nchunks, nblk, bps),
        grid=(nblk + 1,),
        in_specs=[pl.BlockSpec((1, rb, d), lambda j: (*cur(j), 0)),
                  pl.BlockSpec((1, rb, d), lambda j: (*prev(j), 0))]
        + [_const_spec(c.shape) for c in consts],
        out_specs=[pl.BlockSpec((1, rb, d), lambda j: (*prev(j), 0)),
                   pl.BlockSpec((1, GLA_HEADS, GLA_DK, GLA_DV), lambda j: (cur(j)[0], 0, 0, 0))],
        out_shape=[jax.ShapeDtypeStruct((b, t, d), F32),
                   jax.ShapeDtypeStruct((b, GLA_HEADS, GLA_DK, GLA_DV), F32)],
        scratch_shapes=[pltpu.VMEM((rb, GLA_KW), F32), pltpu.VMEM((rb, GLA_KW), F32),
                        pltpu.VMEM((rb, GLA_KW), F32), pltpu.VMEM((rb, GLA_VW), BF16),
                        pltpu.VMEM((rb, GLA_VW), F32), pltpu.VMEM((2, rb, GLA_VW), BF16),
                        pltpu.VMEM((GLA_HEADS, GLA_DK, GLA_DV), F32)],
        compiler_params=pltpu.CompilerParams(dimension_semantics=("arbitrary",),
                                             vmem_limit_bytes=VMEM_LIMIT),
        name="prompt_trunk",
    )(x, x, *consts)


def _gla_sample_heads(mild, nseq, tlen, q_ref, k_ref, v_ref, gate_ref, cum_s, gon_ref, s_in_ref,
                      og_ref, s_out_ref, rxc, lower):
    r = nseq * tlen
    block_mask = (rxc < tlen) & lower
    seq_of_lane = lax.broadcasted_iota(jnp.int32, (GLA_DK, r), 1) >> 3
    for h in range(GLA_HEADS):
        ks = slice(h * GLA_DK, (h + 1) * GLA_DK)
        vs = slice(h * GLA_DV, (h + 1) * GLA_DV)
        q, k, cum = q_ref[:, ks], k_ref[:, ks], cum_s[:, ks]
        vb = v_ref[:, vs]
        qin = (q * jnp.exp(cum)).astype(BF16)
        if mild:
            scores = _intra_scores_direct(qin, k, cum, block_mask)
        else:
            scores = _intra_scores_factored(q, k, cum, (), rxc, lower)
        o_intra = _dot(scores.astype(BF16), vb)
        kd_t = (k * jnp.exp(_sublane_bcast(cum, tlen - 1, tlen) - cum)).T
        cum_t = cum.T
        for i in range(nseq):
            rs = slice(i * tlen, (i + 1) * tlen)
            state = s_in_ref[i, h]
            o = o_intra[rs] + _dot(qin[rs], state.astype(BF16))
            og_ref[rs, vs] = _out_norm_gate(o, gon_ref[...], gate_ref[rs, vs])
            last = cum_t[:, (i + 1) * tlen - 1:(i + 1) * tlen]
            kd_i = jnp.where(seq_of_lane == i, kd_t, 0.0).astype(BF16)
            s_out_ref[i, h] = jnp.exp(last) * state + _dot(kd_i, vb)


def _gla_sample_kernel(nseq, tlen, x_ref, g_ref, w_ref, wa_ref, ba_ref, gon_ref, s_in_ref,
                       og_ref, s_out_ref, q_s, k_s, v_s, gate_s, cum_s):
    hb = _rms(x_ref[...], g_ref[...]).astype(BF16)
    la = _log_decay(_dot(hb, w_ref[:, _A0:GLA_IN_PAD]), wa_ref, ba_ref)
    q_s[...] = _dot(hb, w_ref[:, _Q0:_K0]) * (GLA_DK ** -0.5)
    k_s[...] = _dot(hb, w_ref[:, _K0:_V0])
    rxc, lower = _index_masks(nseq * tlen)
    cum = _masked_prefix_sum(((rxc < tlen) & lower).astype(BF16), la)
    cum_s[...] = cum
    mild = jnp.min(cum) > -DECAY_GUARD
    v_s[...] = _dot(hb, w_ref[:, _V0:_G0]).astype(BF16)
    gate_s[...] = _silu(_dot(hb, w_ref[:, _G0:_A0]))
    args = (nseq, tlen, q_s, k_s, v_s, gate_s, cum_s, gon_ref, s_in_ref, og_ref, s_out_ref, rxc, lower)
    _gla_sample_heads(True, *args)
    pl.when(jnp.logical_not(mild))(functools.partial(_gla_sample_heads, False, *args))


def _gla_sample(x2d, p, state, nseq, tlen):
    n = x2d.shape[0]
    r = nseq * tlen
    assert tlen == SUB and r % CHUNK == 0
    row = lambda w: pl.BlockSpec((r, w), lambda i: (i, 0))
    st = pl.BlockSpec((nseq, GLA_HEADS, GLA_DK, GLA_DV), lambda i: (i, 0, 0, 0))
    consts = [p["g0"], p["w_in"], p["wa"], p["ba"], p["gon"]]
    return pl.pallas_call(
        functools.partial(_gla_sample_kernel, nseq, tlen),
        grid=(n // r,),
        in_specs=[row(D_MODEL)] + [_const_spec(c.shape) for c in consts] + [st],
        out_specs=[row(GLA_VW), st],
        out_shape=[jax.ShapeDtypeStruct((n, GLA_VW), BF16),
                   jax.ShapeDtypeStruct(state.shape, F32)],
        scratch_shapes=[pltpu.VMEM((r, GLA_KW), F32), pltpu.VMEM((r, GLA_KW), F32),
                        pltpu.VMEM((r, GLA_VW), BF16), pltpu.VMEM((r, GLA_VW), F32),
                        pltpu.VMEM((r, GLA_KW), F32)],
        compiler_params=pltpu.CompilerParams(dimension_semantics=("arbitrary",),
                                             vmem_limit_bytes=VMEM_LIMIT),
        name="gla_sample",
    )(x2d, *consts, state)


def _out_cmlp_kernel(nchunks, mix_block, x_ref, og_ref, wo_ref, g1_ref, wi_ref, lng_ref, lnb_ref, ws_ref,
                     bs_ref, wo2_ref, gf_ref, y_ref, v_out_ref):
    x1 = x_ref[...] + _dot(og_ref[...], wo_ref[:, 0:D_MODEL])
    y_ref[...] = _drain(_cmlp_steps(x1, nchunks, mix_block, g1_ref, wi_ref, lng_ref, lnb_ref, ws_ref, bs_ref,
                                    wo2_ref, gf_ref, v_out_ref))


def _out_cmlp(x2d, og, p, ws_eff, bs_full, mix_block, nchunks):
    n = x2d.shape[0]
    rb = nchunks * CHUNK
    row = lambda w: pl.BlockSpec((rb, w), lambda i: (i, 0))
    consts = [p["wo"], p["g1"], p["wi"], p["lng"], p["lnb"], ws_eff, bs_full, p["wo2"], p["gf"]]
    return pl.pallas_call(
        functools.partial(_out_cmlp_kernel, nchunks, mix_block),
        grid=(n // rb,),
        in_specs=[row(D_MODEL), row(GLA_VW)] + [_const_spec(c.shape) for c in consts],
        out_specs=[row(D_MODEL), row(CMLP_WIDTH)],
        out_shape=[jax.ShapeDtypeStruct((n, D_MODEL), F32),
                   jax.ShapeDtypeStruct((n, CMLP_WIDTH), F32)],
        compiler_params=pltpu.CompilerParams(dimension_semantics=("arbitrary",),
                                             vmem_limit_bytes=VMEM_LIMIT),
        name="out_cmlp",
    )(x2d, og, *consts)


def kernel(x_prompt, x_sample, state_gla, norm_g, gla_w_in, gla_w_a_up, gla_b_a_up, gla_g_onorm,
           gla_w_out, cmlp_w_in, cmlp_ln_g, cmlp_ln_b, cmlp_w_spatial, cmlp_b_spatial, cmlp_w_out,
           norm_final):
    b, t, d = x_prompt.shape
    nb, nt, _ = x_sample.shape
    assert t % (PROMPT_CHUNKS_PER_STEP * CHUNK) == 0 and (nb * nt) % CHUNK == 0 and CHUNK % nt == 0

    w_in = _stage_transposed(gla_w_in[0].T, GLA_IN_PAD)
    wo, wi, wo2 = _stage_weights(gla_w_out[0], cmlp_w_in[0], cmlp_w_out[0])
    p = {
        "g0": norm_g[0].reshape(1, d),
        "g1": norm_g[1].reshape(1, d),
        "w_in": w_in,
        "wa": jnp.pad(gla_w_a_up[0], ((0, LANE - GLA_GATE_RANK), (0, 0))).astype(BF16),
        "ba": gla_b_a_up[0].reshape(1, GLA_KW),
        "gon": gla_g_onorm[0].reshape(1, GLA_DV),
        "wo": wo,
        "wi": wi,
        "lng": cmlp_ln_g[0].reshape(1, CMLP_WIDTH),
        "lnb": cmlp_ln_b[0].reshape(1, CMLP_WIDTH),
        "wo2": wo2,
        "gf": norm_final.reshape(1, d),
    }
    ws = cmlp_w_spatial[0]
    bs = cmlp_b_spatial[0]

    def spatial_params(c):
        reps = CHUNK // c
        ws_eff = jnp.tile(ws[:, :c, :c], (1, reps, reps))
        bs_full = jnp.repeat(jnp.tile(bs[:, :c], (1, reps)).T, CMLP_GROUP_DIM, axis=1)
        return ws_eff, bs_full

    y_prompt, s_prompt = _prompt_trunk(x_prompt, p, *spatial_params(CHUNK), PROMPT_CHUNKS_PER_STEP)

    xs = x_sample.reshape(nb * nt, d)
    og, s_sample = _gla_sample(xs, p, state_gla[0], SAMPLE_SEQS_PER_STEP, nt)
    y_sample, v_rows = _out_cmlp(xs, og, p, *spatial_params(nt), nt, SAMPLE_CHUNKS_PER_STEP)

    return (y_prompt, y_sample.reshape(nb, nt, d), s_prompt[None], s_sample[None],
            v_rows.reshape(1, nb, nt, CMLP_WIDTH))
```

```python
import functools

import jax
import jax.numpy as jnp
from jax import lax
from jax.experimental import pallas as pl
from jax.experimental.pallas import tpu as pltpu

F32 = jnp.float32
BF16 = jnp.bfloat16

D_MODEL = 1024
GLA_HEADS = 4
GLA_DK = 128
GLA_DV = 256
GLA_KW = GLA_HEADS * GLA_DK
GLA_VW = GLA_HEADS * GLA_DV
GLA_GATE_RANK = 16
GLA_GATE_TAU = 16.0
CMLP_WIDTH = 1024
CMLP_GROUPS = 4
CMLP_GROUP_DIM = CMLP_WIDTH // CMLP_GROUPS
LANE = 128
CHUNK = 128
SUB = 8
PROMPT_CHUNKS_PER_STEP = 4
SAMPLE_CHUNKS_PER_STEP = 4
SAMPLE_SEQS_PER_STEP = 16
EPS = 1e-6
VMEM_LIMIT = 56 * 1024 * 1024
DECAY_GUARD = 60.0

_Q0, _K0, _V0, _G0, _A0 = 0, GLA_KW, 2 * GLA_KW, 2 * GLA_KW + GLA_VW, 2 * GLA_KW + 2 * GLA_VW
GLA_IN_PAD = _A0 + LANE


def _odd_tile_cols(cols):
    tiles = -(-cols // LANE)
    return (tiles + 1 - tiles % 2) * LANE


def _stage_weights_kernel(*refs):
    n = len(refs) // 2
    for src, dst in zip(refs[:n], refs[n:]):
        cols = src.shape[1]
        dst[:, 0:cols] = src[...].astype(BF16)
        if dst.shape[1] > cols:
            dst[:, cols:] = jnp.zeros((dst.shape[0], dst.shape[1] - cols), BF16)


def _stage_transposed_kernel(valid_rows, src, dst):
    i = pl.program_id(0)
    row = lax.broadcasted_iota(jnp.int32, src.shape, 0) + i * src.shape[0]
    dst[...] = jnp.where(row < valid_rows, src[...], 0.0).T.astype(BF16)


def _stage_transposed(w_t, cols):
    n, rows = w_t.shape
    blk = 5 * LANE
    assert cols % blk == 0 and cols >= n
    return pl.pallas_call(
        functools.partial(_stage_transposed_kernel, n),
        grid=(cols // blk,),
        in_specs=[pl.BlockSpec((blk, rows), lambda i: (i, 0))],
        out_specs=pl.BlockSpec((rows, blk), lambda i: (0, i)),
        out_shape=jax.ShapeDtypeStruct((rows, cols), BF16),
        compiler_params=pltpu.CompilerParams(dimension_semantics=("arbitrary",),
                                             vmem_limit_bytes=VMEM_LIMIT),
        name="stage_transposed",
    )(w_t)


def _stage_weights(*weights):
    rows = weights[0].shape[0]
    assert all(w.shape[0] == rows for w in weights)
    outs = [_odd_tile_cols(w.shape[1]) for w in weights]
    blk = lambda c: pl.BlockSpec((CHUNK, c), lambda i: (i, 0))
    return pl.pallas_call(
        _stage_weights_kernel,
        grid=(rows // CHUNK,),
        in_specs=[blk(w.shape[1]) for w in weights],
        out_specs=[blk(c) for c in outs],
        out_shape=[jax.ShapeDtypeStruct((rows, c), BF16) for c in outs],
        compiler_params=pltpu.CompilerParams(dimension_semantics=("arbitrary",),
                                             vmem_limit_bytes=VMEM_LIMIT),
        name="stage_weights",
    )(*weights)


def _dot(a, b):
    return jnp.dot(a, b, preferred_element_type=F32)


def _dot_nt(a, b):
    return lax.dot_general(a, b, (((1,), (1,)), ((), ())), preferred_element_type=F32)


def _rms(x, g):
    return x * lax.rsqrt(jnp.mean(x * x, axis=-1, keepdims=True) + EPS) * g


def _silu(x):
    return x * jax.nn.sigmoid(x)


def _masked_prefix_sum(mask01, x):
    hi = x.astype(BF16)
    mid = (x - hi.astype(F32)).astype(BF16)
    return _dot(jnp.concatenate([mask01, mask01], axis=1), jnp.concatenate([hi, mid], axis=0))


def _sublane_bcast(x, row, group):
    r, n = x.shape
    x3 = x.reshape(r // group, group, n)
    return jnp.broadcast_to(x3[:, row:row + 1, :], x3.shape).reshape(r, n)


def _index_masks(r):
    row = lax.broadcasted_iota(jnp.int32, (r, r), 0)
    col = lax.broadcasted_iota(jnp.int32, (r, r), 1)
    return row ^ col, col <= row


def _log_decay(a_low, wa_ref, ba_ref):
    z = _dot(a_low.astype(BF16), wa_ref[...]) + ba_ref[...]
    log_sig = jnp.minimum(z, 0.0) - jnp.log1p(jnp.exp(-jnp.abs(z)))
    return log_sig * (1.0 / GLA_GATE_TAU)


def _layernorm(v, g, b):
    vc = v - jnp.mean(v, axis=-1, keepdims=True)
    var = jnp.mean(vc * vc, axis=-1, keepdims=True)
    return vc * lax.rsqrt(var + EPS) * g + b


def _intra_scores_factored(q, k, cum, levels, rxc, lower):
    r = q.shape[0]
    parts = []
    for s in range(SUB):
        kb = _sublane_bcast(k, s, SUB)
        cb = _sublane_bcast(cum, s, SUB)
        parts.append((q * kb * jnp.exp(jnp.minimum(cum - cb, 0.0))).astype(BF16))
    acat = jnp.concatenate(parts, axis=1)
    pr = lax.broadcasted_iota(jnp.int32, (SUB * GLA_DK, r), 0)
    pc = lax.broadcasted_iota(jnp.int32, (SUB * GLA_DK, r), 1)
    lane_sum = ((pc & (SUB - 1)) == (pr >> 7)).astype(BF16)
    scores = jnp.where((rxc < SUB) & lower, _dot(acat, lane_sum), 0.0)
    for m in levels:
        qk = jnp.concatenate([(q if (i // m) % 2 else k)[i:i + m] for i in range(0, r, m)], axis=0)
        ref = _sublane_bcast(cum, m - 1, 2 * m)
        f = (qk * jnp.exp(-jnp.abs(cum - ref))).astype(BF16)
        scores = jnp.where((rxc >= m) & (rxc < 2 * m) & lower, _dot_nt(f, f), scores)
    return scores


def _intra_scores_direct(qin, k, cum, mask):
    kout = (k * jnp.exp(-cum)).astype(BF16)
    return jnp.where(mask, _dot_nt(qin, kout), 0.0)


def _out_norm_gate(o, gon, gate):
    return (_rms(o, gon) * gate).astype(BF16)


GLA_HEAD_STAGES = 3


def _gla_prompt_head(mild, h, rows, q_s, k_s, cum_s, v_s, gate_s, og_ref, s_in, s_out, gon_ref, rxc,
                     lower):
    ks = slice(h * GLA_DK, (h + 1) * GLA_DK)
    vs = slice(h * GLA_DV, (h + 1) * GLA_DV)
    q, k, cum = q_s[rows, ks], k_s[rows, ks], cum_s[rows, ks]
    vb = v_s[rows, vs]
    qin = (q * jnp.exp(cum)).astype(BF16)
    if mild:
        scores = _intra_scores_direct(qin, k, cum, lower)
    else:
        scores = _intra_scores_factored(q, k, cum, (64, 32, 16, 8), rxc, lower)
    k_t, cum_t = k.T, cum.T
    last = cum_t[:, CHUNK - 1:CHUNK]
    kd_t = (k_t * jnp.exp(last - cum_t)).astype(BF16)
    s_add = _dot(kd_t, vb)
    yield None
    state = s_in[h]
    o = _dot(jnp.concatenate([scores.astype(BF16), qin], axis=1),
             jnp.concatenate([vb, state.astype(BF16)], axis=0))
    s_out[h] = jnp.exp(last) * state + s_add
    yield None
    og_ref[rows, vs] = _out_norm_gate(o, gon_ref[...], gate_s[rows, vs])
    yield None


def _gla_prompt_chunk_stages(mild, c, *args):
    rows = slice(c * CHUNK, (c + 1) * CHUNK)
    gens = [_gla_prompt_head(mild, h, rows, *args) for h in range(GLA_HEADS)]
    for _ in range(GLA_HEAD_STAGES):
        for g in gens:
            next(g)
        yield None


def _interleave(first, n_first, second, n_second):
    last = None
    done_second = 0
    for i in range(n_first):
        next(first)
        while done_second * n_first < (i + 1) * n_second:
            last = next(second)
            done_second += 1
    return last


def _cmlp_steps(x1, nchunks, mix_block, g1_ref, wi_ref, lng_ref, lnb_ref, ws_ref, bs_ref, wo2_ref,
                gf_ref, v_out_ref=None):
    w = CMLP_WIDTH
    hb = _rms(x1, g1_ref[...]).astype(BF16)
    yield None
    u = _dot(hb, wi_ref[:, 0:w])
    yield None
    vn = _layernorm(_dot(hb, wi_ref[:, w:2 * w]), lng_ref[...], lnb_ref[...])
    if v_out_ref is not None:
        v_out_ref[...] = vn
    yield None
    ug = u * _silu(_dot(hb, wi_ref[:, 2 * w:3 * w]))
    yield None
    vnb = vn.astype(BF16)
    rxc, lower = _index_masks(CHUNK)
    mix_mask = (rxc < mix_block) & lower
    zs = []
    for g in range(CMLP_GROUPS):
        gs = slice(g * CMLP_GROUP_DIM, (g + 1) * CMLP_GROUP_DIM)
        wsm = jnp.where(mix_mask, ws_ref[g], 0.0).astype(BF16)
        mixed = jnp.concatenate(
            [_dot(wsm, vnb[c * CHUNK:(c + 1) * CHUNK, gs]) + bs_ref[:, gs] for c in range(nchunks)],
            axis=0)
        zs.append((ug[:, gs] * mixed).astype(BF16))
    yield None
    acc = x1
    for g in range(CMLP_GROUPS):
        gs = slice(g * CMLP_GROUP_DIM, (g + 1) * CMLP_GROUP_DIM)
        acc = acc + _dot(zs[g], wo2_ref[gs, 0:D_MODEL])
        if g < CMLP_GROUPS - 1:
            yield None
    yield _rms(acc, gf_ref[...])


CMLP_STEPS = 9


def _drain(gen):
    out = None
    for out in gen:
        pass
    return out


def _prompt_kernel(nchunks, nblk, blk_per_seq, xa_ref, xc_ref, g0_ref, win_ref, wa_ref, ba_ref,
                   gon_ref, wo_ref, g1_ref, wi_ref, lng_ref, lnb_ref, ws_ref, bs_ref, wo2_ref, gf_ref,
                   y_ref, s_ref, q_s, k_s, cum_s, v_s, gate_s, og_s, st_s):
    j = pl.program_id(0)
    valid = j < nblk
    jj = jnp.minimum(j, nblk - 1)
    is_first = lax.rem(jj, blk_per_seq) == 0
    slot = lax.rem(j, 2)
    og_cur, og_prev = og_s.at[slot], og_s.at[1 - slot]

    @pl.when(j == 0)
    def _():
        og_s[...] = jnp.zeros_like(og_s)

    @pl.when(is_first)
    def _():
        s_ref[...] = jnp.zeros_like(s_ref)

    x1 = xc_ref[0] + _dot(og_prev[...], wo_ref[:, 0:D_MODEL])
    hb = _rms(xa_ref[0], g0_ref[...]).astype(BF16)
    half = hb.shape[0] // 2
    a_low = jnp.concatenate([_dot(hb[:half], win_ref[:, _A0:GLA_IN_PAD]),
                             _dot(hb[half:], win_ref[:, _A0:GLA_IN_PAD])], axis=0)
    la = _log_decay(a_low, wa_ref, ba_ref)
    q_s[...] = _dot(hb, win_ref[:, _Q0:_K0]) * (GLA_DK ** -0.5)
    k_s[...] = _dot(hb, win_ref[:, _K0:_V0])
    rxc, lower = _index_masks(CHUNK)
    tri = lower.astype(BF16)
    total = None
    for c in range(nchunks):
        rows = slice(c * CHUNK, (c + 1) * CHUNK)
        cum = _masked_prefix_sum(tri, la[rows])
        cum_s[rows, :] = cum
        end = jnp.min(cum[CHUNK - SUB:CHUNK, :])
        total = end if total is None else jnp.minimum(total, end)
    mild = total > -DECAY_GUARD
    cmlp = _cmlp_steps(x1, nchunks, CHUNK, g1_ref, wi_ref, lng_ref, lnb_ref, ws_ref, bs_ref, wo2_ref,
                       gf_ref)
    next(cmlp)
    v_s[...] = _dot(hb, win_ref[:, _V0:_G0]).astype(BF16)
    gate_s[...] = _silu(_dot(hb, win_ref[:, _G0:_A0]))

    def recurrence(is_mild, carried):
        for c in range(nchunks):
            yield from _gla_prompt_chunk_stages(
                is_mild, c, q_s, k_s, cum_s, v_s, gate_s, og_cur,
                s_ref.at[0] if c == 0 else carried, carried, gon_ref, rxc, lower)

    y_ref[0] = _interleave(recurrence(True, st_s), nchunks * GLA_HEAD_STAGES, cmlp, CMLP_STEPS - 1)
    commit = jnp.logical_and(valid, mild)
    for h in range(GLA_HEADS):
        s_ref[0, h] = jnp.where(commit, st_s[h], s_ref[0, h])

    @pl.when(jnp.logical_and(valid, jnp.logical_not(mild)))
    def _():
        _drain(recurrence(False, s_ref.at[0]))


def _const_spec(shape):
    nd = len(shape)
    return pl.BlockSpec(shape, lambda *_: (0,) * nd, pipeline_mode=pl.Buffered(1))


def _prompt_trunk(x, p, ws_eff, bs_full, nchunks):
    b, t, d = x.shape
    rb = nchunks * CHUNK
    bps = t // rb
    nblk = b * bps
    consts = [p["g0"], p["w_in"], p["wa"], p["ba"], p["gon"], p["wo"], p["g1"], p["wi"], p["lng"],
              p["lnb"], ws_eff, bs_full, p["wo2"], p["gf"]]

    def cur(j):
        jj = jnp.minimum(j, nblk - 1)
        return jj // bps, jj % bps

    def prev(j):
        jj = jnp.maximum(j - 1, 0)
        return jj // bps, jj % bps

    return pl.pallas_call(
        functools.partial(_prompt_kernel, nchunks, nblk, bps),
        grid=(nblk + 1,),
        in_specs=[pl.BlockSpec((1, rb, d), lambda j: (*cur(j), 0)),
                  pl.BlockSpec((1, rb, d), lambda j: (*prev(j), 0))]
        + [_const_spec(c.shape) for c in consts],
        out_specs=[pl.BlockSpec((1, rb, d), lambda j: (*prev(j), 0)),
                   pl.BlockSpec((1, GLA_HEADS, GLA_DK, GLA_DV), lambda j: (cur(j)[0], 0, 0, 0))],
        out_shape=[jax.ShapeDtypeStruct((b, t, d), F32),
                   jax.ShapeDtypeStruct((b, GLA_HEADS, GLA_DK, GLA_DV), F32)],
        scratch_shapes=[pltpu.VMEM((rb, GLA_KW), F32), pltpu.VMEM((rb, GLA_KW), F32),
                        pltpu.VMEM((rb, GLA_KW), F32), pltpu.VMEM((rb, GLA_VW), BF16),
                        pltpu.VMEM((rb, GLA_VW), F32), pltpu.VMEM((2, rb, GLA_VW), BF16),
                        pltpu.VMEM((GLA_HEADS, GLA_DK, GLA_DV), F32)],
        compiler_params=pltpu.CompilerParams(dimension_semantics=("arbitrary",),
                                             vmem_limit_bytes=VMEM_LIMIT),
        name="prompt_trunk",
    )(x, x, *consts)


def _gla_sample_heads(mild, nseq, tlen, q_ref, k_ref, v_ref, gate_ref, cum_s, gon_ref, s_in_ref,
                      og_ref, s_out_ref, rxc, lower):
    r = nseq * tlen
    block_mask = (rxc < tlen) & lower
    seq_of_lane = lax.broadcasted_iota(jnp.int32, (GLA_DK, r), 1) >> 3
    for h in range(GLA_HEADS):
        ks = slice(h * GLA_DK, (h + 1) * GLA_DK)
        vs = slice(h * GLA_DV, (h + 1) * GLA_DV)
        q, k, cum = q_ref[:, ks], k_ref[:, ks], cum_s[:, ks]
        vb = v_ref[:, vs]
        qin = (q * jnp.exp(cum)).astype(BF16)
        if mild:
            scores = _intra_scores_direct(qin, k, cum, block_mask)
        else:
            scores = _intra_scores_factored(q, k, cum, (), rxc, lower)
        o_intra = _dot(scores.astype(BF16), vb)
        kd_t = (k * jnp.exp(_sublane_bcast(cum, tlen - 1, tlen) - cum)).T
        cum_t = cum.T
        for i in range(nseq):
            rs = slice(i * tlen, (i + 1) * tlen)
            state = s_in_ref[i, h]
            o = o_intra[rs] + _dot(qin[rs], state.astype(BF16))
            og_ref[rs, vs] = _out_norm_gate(o, gon_ref[...], gate_ref[rs, vs])
            last = cum_t[:, (i + 1) * tlen - 1:(i + 1) * tlen]
            kd_i = jnp.where(seq_of_lane == i, kd_t, 0.0).astype(BF16)
            s_out_ref[i, h] = jnp.exp(last) * state + _dot(kd_i, vb)


def _gla_sample_kernel(nseq, tlen, x_ref, g_ref, w_ref, wa_ref, ba_ref, gon_ref, s_in_ref,
                       og_ref, s_out_ref, q_s, k_s, v_s, gate_s, cum_s):
    hb = _rms(x_ref[...], g_ref[...]).astype(BF16)
    la = _log_decay(_dot(hb, w_ref[:, _A0:GLA_IN_PAD]), wa_ref, ba_ref)
    q_s[...] = _dot(hb, w_ref[:, _Q0:_K0]) * (GLA_DK ** -0.5)
    k_s[...] = _dot(hb, w_ref[:, _K0:_V0])
    rxc, lower = _index_masks(nseq * tlen)
    cum = _masked_prefix_sum(((rxc < tlen) & lower).astype(BF16), la)
    cum_s[...] = cum
    mild = jnp.min(cum) > -DECAY_GUARD
    v_s[...] = _dot(hb, w_ref[:, _V0:_G0]).astype(BF16)
    gate_s[...] = _silu(_dot(hb, w_ref[:, _G0:_A0]))
    args = (nseq, tlen, q_s, k_s, v_s, gate_s, cum_s, gon_ref, s_in_ref, og_ref, s_out_ref, rxc, lower)
    _gla_sample_heads(True, *args)
    pl.when(jnp.logical_not(mild))(functools.partial(_gla_sample_heads, False, *args))


def _gla_sample(x2d, p, state, nseq, tlen):
    n = x2d.shape[0]
    r = nseq * tlen
    assert tlen == SUB and r % CHUNK == 0
    row = lambda w: pl.BlockSpec((r, w), lambda i: (i, 0))
    st = pl.BlockSpec((nseq, GLA_HEADS, GLA_DK, GLA_DV), lambda i: (i, 0, 0, 0))
    consts = [p["g0"], p["w_in"], p["wa"], p["ba"], p["gon"]]
    return pl.pallas_call(
        functools.partial(_gla_sample_kernel, nseq, tlen),
        grid=(n // r,),
        in_specs=[row(D_MODEL)] + [_const_spec(c.shape) for c in consts] + [st],
        out_specs=[row(GLA_VW), st],
        out_shape=[jax.ShapeDtypeStruct((n, GLA_VW), BF16),
                   jax.ShapeDtypeStruct(state.shape, F32)],
        scratch_shapes=[pltpu.VMEM((r, GLA_KW), F32), pltpu.VMEM((r, GLA_KW), F32),
                        pltpu.VMEM((r, GLA_VW), BF16), pltpu.VMEM((r, GLA_VW), F32),
                        pltpu.VMEM((r, GLA_KW), F32)],
        compiler_params=pltpu.CompilerParams(dimension_semantics=("arbitrary",),
                                             vmem_limit_bytes=VMEM_LIMIT),
        name="gla_sample",
    )(x2d, *consts, state)


def _out_cmlp_kernel(nchunks, mix_block, x_ref, og_ref, wo_ref, g1_ref, wi_ref, lng_ref, lnb_ref, ws_ref,
                     bs_ref, wo2_ref, gf_ref, y_ref, v_out_ref):
    x1 = x_ref[...] + _dot(og_ref[...], wo_ref[:, 0:D_MODEL])
    y_ref[...] = _drain(_cmlp_steps(x1, nchunks, mix_block, g1_ref, wi_ref, lng_ref, lnb_ref, ws_ref, bs_ref,
                                    wo2_ref, gf_ref, v_out_ref))


def _out_cmlp(x2d, og, p, ws_eff, bs_full, mix_block, nchunks):
    n = x2d.shape[0]
    rb = nchunks * CHUNK
    row = lambda w: pl.BlockSpec((rb, w), lambda i: (i, 0))
    consts = [p["wo"], p["g1"], p["wi"], p["lng"], p["lnb"], ws_eff, bs_full, p["wo2"], p["gf"]]
    return pl.pallas_call(
        functools.partial(_out_cmlp_kernel, nchunks, mix_block),
        grid=(n // rb,),
        in_specs=[row(D_MODEL), row(GLA_VW)] + [_const_spec(c.shape) for c in consts],
        out_specs=[row(D_MODEL), row(CMLP_WIDTH)],
        out_shape=[jax.ShapeDtypeStruct((n, D_MODEL), F32),
                   jax.ShapeDtypeStruct((n, CMLP_WIDTH), F32)],
        compiler_params=pltpu.CompilerParams(dimension_semantics=("arbitrary",),
                                             vmem_limit_bytes=VMEM_LIMIT),
        name="out_cmlp",
    )(x2d, og, *consts)


def kernel(x_prompt, x_sample, state_gla, norm_g, gla_w_in, gla_w_a_up, gla_b_a_up, gla_g_onorm,
           gla_w_out, cmlp_w_in, cmlp_ln_g, cmlp_ln_b, cmlp_w_spatial, cmlp_b_spatial, cmlp_w_out,
           norm_final):
    b, t, d = x_prompt.shape
    nb, nt, _ = x_sample.shape
    assert t % (PROMPT_CHUNKS_PER_STEP * CHUNK) == 0 and (nb * nt) % CHUNK == 0 and CHUNK % nt == 0

    w_in = _stage_transposed(gla_w_in[0].T, GLA_IN_PAD)
    wo, wi, wo2 = _stage_weights(gla_w_out[0], cmlp_w_in[0], cmlp_w_out[0])
    p = {
        "g0": norm_g[0].reshape(1, d),
        "g1": norm_g[1].reshape(1, d),
        "w_in": w_in,
        "wa": jnp.pad(gla_w_a_up[0], ((0, LANE - GLA_GATE_RANK), (0, 0))).astype(BF16),
        "ba": gla_b_a_up[0].reshape(1, GLA_KW),
        "gon": gla_g_onorm[0].reshape(1, GLA_DV),
        "wo": wo,
        "wi": wi,
        "lng": cmlp_ln_g[0].reshape(1, CMLP_WIDTH),
        "lnb": cmlp_ln_b[0].reshape(1, CMLP_WIDTH),
        "wo2": wo2,
        "gf": norm_final.reshape(1, d),
    }
    ws = cmlp_w_spatial[0]
    bs = cmlp_b_spatial[0]

    def spatial_params(c):
        reps = CHUNK // c
        ws_eff = jnp.tile(ws[:, :c, :c], (1, reps, reps))
        bs_full = jnp.repeat(jnp.tile(bs[:, :c], (1, reps)).T, CMLP_GROUP_DIM, axis=1)
        return ws_eff, bs_full

    y_prompt, s_prompt = _prompt_trunk(x_prompt, p, *spatial_params(CHUNK), PROMPT_CHUNKS_PER_STEP)

    xs = x_sample.reshape(nb * nt, d)
    og, s_sample = _gla_sample(xs, p, state_gla[0], SAMPLE_SEQS_PER_STEP, nt)
    y_sample, v_rows = _out_cmlp(xs, og, p, *spatial_params(nt), nt, SAMPLE_CHUNKS_PER_STEP)

    return (y_prompt, y_sample.reshape(nb, nt, d), s_prompt[None], s_sample[None],
            v_rows.reshape(1, nb, nt, CMLP_WIDTH))
```

```python
import functools

import jax
import jax.numpy as jnp
from jax import lax
from jax.experimental import pallas as pl
from jax.experimental.pallas import tpu as pltpu

F32 = jnp.float32
BF16 = jnp.bfloat16

D_MODEL = 1024
GLA_HEADS = 4
GLA_DK = 128
GLA_DV = 256
GLA_KW = GLA_HEADS * GLA_DK
GLA_VW = GLA_HEADS * GLA_DV
GLA_GATE_RANK = 16
GLA_GATE_TAU = 16.0
CMLP_WIDTH = 1024
CMLP_GROUPS = 4
CMLP_GROUP_DIM = CMLP_WIDTH // CMLP_GROUPS
LANE = 128
CHUNK = 128
SUB = 8
PROMPT_CHUNKS_PER_STEP = 4
SAMPLE_CHUNKS_PER_STEP = 4
SAMPLE_SEQS_PER_STEP = 16
EPS = 1e-6
VMEM_LIMIT = 56 * 1024 * 1024
DECAY_GUARD = 60.0

_Q0, _K0, _V0, _G0, _A0 = 0, GLA_KW, 2 * GLA_KW, 2 * GLA_KW + GLA_VW, 2 * GLA_KW + 2 * GLA_VW
GLA_IN_PAD = _A0 + LANE


def _odd_tile_cols(cols):
    tiles = -(-cols // LANE)
    return (tiles + 1 - tiles % 2) * LANE


def _stage_weights_kernel(*refs):
    n = len(refs) // 2
    for src, dst in zip(refs[:n], refs[n:]):
        cols = src.shape[1]
        dst[:, 0:cols] = src[...].astype(BF16)
        if dst.shape[1] > cols:
            dst[:, cols:] = jnp.zeros((dst.shape[0], dst.shape[1] - cols), BF16)


def _stage_transposed_kernel(valid_rows, src, dst):
    i = pl.program_id(0)
    row = lax.broadcasted_iota(jnp.int32, src.shape, 0) + i * src.shape[0]
    dst[...] = jnp.where(row < valid_rows, src[...], 0.0).T.astype(BF16)


def _stage_transposed(w_t, cols):
    n, rows = w_t.shape
    blk = 5 * LANE
    assert cols % blk == 0 and cols >= n
    return pl.pallas_call(
        functools.partial(_stage_transposed_kernel, n),
        grid=(cols // blk,),
        in_specs=[pl.BlockSpec((blk, rows), lambda i: (i, 0))],
        out_specs=pl.BlockSpec((rows, blk), lambda i: (0, i)),
        out_shape=jax.ShapeDtypeStruct((rows, cols), BF16),
        compiler_params=pltpu.CompilerParams(dimension_semantics=("arbitrary",),
                                             vmem_limit_bytes=VMEM_LIMIT),
        name="stage_transposed",
    )(w_t)


def _stage_weights(*weights):
    rows = weights[0].shape[0]
    assert all(w.shape[0] == rows for w in weights)
    outs = [_odd_tile_cols(w.shape[1]) for w in weights]
    blk = lambda c: pl.BlockSpec((CHUNK, c), lambda i: (i, 0))
    return pl.pallas_call(
        _stage_weights_kernel,
        grid=(rows // CHUNK,),
        in_specs=[blk(w.shape[1]) for w in weights],
        out_specs=[blk(c) for c in outs],
        out_shape=[jax.ShapeDtypeStruct((rows, c), BF16) for c in outs],
        compiler_params=pltpu.CompilerParams(dimension_semantics=("arbitrary",),
                                             vmem_limit_bytes=VMEM_LIMIT),
        name="stage_weights",
    )(*weights)


def _dot(a, b):
    return jnp.dot(a, b, preferred_element_type=F32)


def _dot_nt(a, b):
    return lax.dot_general(a, b, (((1,), (1,)), ((), ())), preferred_element_type=F32)


def _rms(x, g):
    return x * lax.rsqrt(jnp.mean(x * x, axis=-1, keepdims=True) + EPS) * g


def _silu(x):
    return x * jax.nn.sigmoid(x)


def _masked_prefix_sum(mask01, x):
    hi = x.astype(BF16)
    mid = (x - hi.astype(F32)).astype(BF16)
    return _dot(jnp.concatenate([mask01, mask01], axis=1), jnp.concatenate([hi, mid], axis=0))


def _sublane_bcast(x, row, group):
    r, n = x.shape
    x3 = x.reshape(r // group, group, n)
    return jnp.broadcast_to(x3[:, row:row + 1, :], x3.shape).reshape(r, n)


def _index_masks(r):
    row = lax.broadcasted_iota(jnp.int32, (r, r), 0)
    col = lax.broadcasted_iota(jnp.int32, (r, r), 1)
    return row ^ col, col <= row


def _log_decay(a_low, wa_ref, ba_ref):
    z = _dot(a_low.astype(BF16), wa_ref[...]) + ba_ref[...]
    log_sig = jnp.minimum(z, 0.0) - jnp.log1p(jnp.exp(-jnp.abs(z)))
    return log_sig * (1.0 / GLA_GATE_TAU)


def _layernorm(v, g, b):
    vc = v - jnp.mean(v, axis=-1, keepdims=True)
    var = jnp.mean(vc * vc, axis=-1, keepdims=True)
    return vc * lax.rsqrt(var + EPS) * g + b


def _intra_scores_factored(q, k, cum, levels, rxc, lower):
    r = q.shape[0]
    parts = []
    for s in range(SUB):
        kb = _sublane_bcast(k, s, SUB)
        cb = _sublane_bcast(cum, s, SUB)
        parts.append((q * kb * jnp.exp(jnp.minimum(cum - cb, 0.0))).astype(BF16))
    acat = jnp.concatenate(parts, axis=1)
    pr = lax.broadcasted_iota(jnp.int32, (SUB * GLA_DK, r), 0)
    pc = lax.broadcasted_iota(jnp.int32, (SUB * GLA_DK, r), 1)
    lane_sum = ((pc & (SUB - 1)) == (pr >> 7)).astype(BF16)
    scores = jnp.where((rxc < SUB) & lower, _dot(acat, lane_sum), 0.0)
    for m in levels:
        qk = jnp.concatenate([(q if (i // m) % 2 else k)[i:i + m] for i in range(0, r, m)], axis=0)
        ref = _sublane_bcast(cum, m - 1, 2 * m)
        f = (qk * jnp.exp(-jnp.abs(cum - ref))).astype(BF16)
        scores = jnp.where((rxc >= m) & (rxc < 2 * m) & lower, _dot_nt(f, f), scores)
    return scores


def _intra_scores_direct(qin, k, cum, mask):
    kout = (k * jnp.exp(-cum)).astype(BF16)
    return jnp.where(mask, _dot_nt(qin, kout), 0.0)


def _out_norm_gate(o, gon, gate):
    return (_rms(o, gon) * gate).astype(BF16)


GLA_HEAD_STAGES = 3


def _gla_prompt_head(mild, h, rows, q_s, k_s, cum_s, v_s, gate_s, og_ref, s_in, s_out, gon_ref, rxc,
                     lower):
    ks = slice(h * GLA_DK, (h + 1) * GLA_DK)
    vs = slice(h * GLA_DV, (h + 1) * GLA_DV)
    q, k, cum = q_s[rows, ks], k_s[rows, ks], cum_s[rows, ks]
    vb = v_s[rows, vs]
    qin = (q * jnp.exp(cum)).astype(BF16)
    if mild:
        scores = _intra_scores_direct(qin, k, cum, lower)
    else:
        scores = _intra_scores_factored(q, k, cum, (64, 32, 16, 8), rxc, lower)
    k_t, cum_t = k.T, cum.T
    last = cum_t[:, CHUNK - 1:CHUNK]
    kd_t = (k_t * jnp.exp(last - cum_t)).astype(BF16)
    s_add = _dot(kd_t, vb)
    yield None
    state = s_in[h]
    o = _dot(jnp.concatenate([scores.astype(BF16), qin], axis=1),
             jnp.concatenate([vb, state.astype(BF16)], axis=0))
    s_out[h] = jnp.exp(last) * state + s_add
    yield None
    og_ref[rows, vs] = _out_norm_gate(o, gon_ref[...], gate_s[rows, vs])
    yield None


def _gla_prompt_chunk_stages(mild, c, *args):
    rows = slice(c * CHUNK, (c + 1) * CHUNK)
    gens = [_gla_prompt_head(mild, h, rows, *args) for h in range(GLA_HEADS)]
    for _ in range(GLA_HEAD_STAGES):
        for g in gens:
            next(g)
        yield None


def _interleave(first, n_first, second, n_second):
    last = None
    done_second = 0
    for i in range(n_first):
        next(first)
        while done_second * n_first < (i + 1) * n_second:
            last = next(second)
            done_second += 1
    return last


def _cmlp_steps(x1, nchunks, mix_block, g1_ref, wi_ref, lng_ref, lnb_ref, ws_ref, bs_ref, wo2_ref,
                gf_ref, v_out_ref=None):
    w = CMLP_WIDTH
    hb = _rms(x1, g1_ref[...]).astype(BF16)
    yield None
    u = _dot(hb, wi_ref[:, 0:w])
    yield None
    vn = _layernorm(_dot(hb, wi_ref[:, w:2 * w]), lng_ref[...], lnb_ref[...])
    if v_out_ref is not None:
        v_out_ref[...] = vn
    yield None
    ug = u * _silu(_dot(hb, wi_ref[:, 2 * w:3 * w]))
    yield None
    vnb = vn.astype(BF16)
    rxc, lower = _index_masks(CHUNK)
    mix_mask = (rxc < mix_block) & lower
    zs = []
    for g in range(CMLP_GROUPS):
        gs = slice(g * CMLP_GROUP_DIM, (g + 1) * CMLP_GROUP_DIM)
        wsm = jnp.where(mix_mask, ws_ref[g], 0.0).astype(BF16)
        mixed = jnp.concatenate(
            [_dot(wsm, vnb[c * CHUNK:(c + 1) * CHUNK, gs]) + bs_ref[:, gs] for c in range(nchunks)],
            axis=0)
        zs.append((ug[:, gs] * mixed).astype(BF16))
    yield None
    blocks = _cmlp_out_blocks(nchunks)
    rows = nchunks * CHUNK // blocks
    ys = []
    for b in range(blocks):
        rs = slice(b * rows, (b + 1) * rows)
        acc = x1[rs]
        for g in range(CMLP_GROUPS):
            gs = slice(g * CMLP_GROUP_DIM, (g + 1) * CMLP_GROUP_DIM)
            acc = acc + _dot(zs[g][rs], wo2_ref[gs, 0:D_MODEL])
            if (b, g) != (blocks - 1, CMLP_GROUPS - 1):
                yield None
        ys.append(_rms(acc, gf_ref[...]))
    yield jnp.concatenate(ys, axis=0)


def _cmlp_out_blocks(nchunks):
    return 2 if nchunks % 2 == 0 else 1


def _cmlp_num_steps(nchunks):
    return 5 + _cmlp_out_blocks(nchunks) * CMLP_GROUPS


def _drain(gen):
    out = None
    for out in gen:
        pass
    return out


def _prompt_kernel(nchunks, nblk, blk_per_seq, xa_ref, xc_ref, g0_ref, win_ref, wa_ref, ba_ref,
                   gon_ref, wo_ref, g1_ref, wi_ref, lng_ref, lnb_ref, ws_ref, bs_ref, wo2_ref, gf_ref,
                   y_ref, s_ref, q_s, k_s, cum_s, v_s, gate_s, og_s, st_s):
    j = pl.program_id(0)
    valid = j < nblk
    jj = jnp.minimum(j, nblk - 1)
    is_first = lax.rem(jj, blk_per_seq) == 0
    slot = lax.rem(j, 2)
    og_cur, og_prev = og_s.at[slot], og_s.at[1 - slot]

    @pl.when(j == 0)
    def _():
        og_s[...] = jnp.zeros_like(og_s)

    @pl.when(is_first)
    def _():
        s_ref[...] = jnp.zeros_like(s_ref)

    x1 = xc_ref[0] + _dot(og_prev[...], wo_ref[:, 0:D_MODEL])
    hb = _rms(xa_ref[0], g0_ref[...]).astype(BF16)
    half = hb.shape[0] // 2
    a_low = jnp.concatenate([_dot(hb[:half], win_ref[:, _A0:GLA_IN_PAD]),
                             _dot(hb[half:], win_ref[:, _A0:GLA_IN_PAD])], axis=0)
    la = _log_decay(a_low, wa_ref, ba_ref)
    q_s[...] = _dot(hb, win_ref[:, _Q0:_K0]) * (GLA_DK ** -0.5)
    k_s[...] = _dot(hb, win_ref[:, _K0:_V0])
    rxc, lower = _index_masks(CHUNK)
    tri = lower.astype(BF16)
    total = None
    for c in range(nchunks):
        rows = slice(c * CHUNK, (c + 1) * CHUNK)
        cum = _masked_prefix_sum(tri, la[rows])
        cum_s[rows, :] = cum
        end = jnp.min(cum[CHUNK - SUB:CHUNK, :])
        total = end if total is None else jnp.minimum(total, end)
    mild = total > -DECAY_GUARD
    cmlp = _cmlp_steps(x1, nchunks, CHUNK, g1_ref, wi_ref, lng_ref, lnb_ref, ws_ref, bs_ref, wo2_ref,
                       gf_ref)
    next(cmlp)
    v_s[...] = _dot(hb, win_ref[:, _V0:_G0]).astype(BF16)
    gate_s[...] = _silu(_dot(hb, win_ref[:, _G0:_A0]))

    def recurrence(is_mild, carried):
        for c in range(nchunks):
            yield from _gla_prompt_chunk_stages(
                is_mild, c, q_s, k_s, cum_s, v_s, gate_s, og_cur,
                s_ref.at[0] if c == 0 else carried, carried, gon_ref, rxc, lower)

    y_ref[0] = _interleave(recurrence(True, st_s), nchunks * GLA_HEAD_STAGES,
                           cmlp, _cmlp_num_steps(nchunks) - 1)
    commit = jnp.logical_and(valid, mild)
    for h in range(GLA_HEADS):
        s_ref[0, h] = jnp.where(commit, st_s[h], s_ref[0, h])

    @pl.when(jnp.logical_and(valid, jnp.logical_not(mild)))
    def _():
        _drain(recurrence(False, s_ref.at[0]))


def _const_spec(shape):
    nd = len(shape)
    return pl.BlockSpec(shape, lambda *_: (0,) * nd, pipeline_mode=pl.Buffered(1))


def _prompt_trunk(x, p, ws_eff, bs_full, nchunks):
    b, t, d = x.shape
    rb = nchunks * CHUNK
    bps = t // rb
    nblk = b * bps
    consts = [p["g0"], p["w_in"], p["wa"], p["ba"], p["gon"], p["wo"], p["g1"], p["wi"], p["lng"],
              p["lnb"], ws_eff, bs_full, p["wo2"], p["gf"]]

    def cur(j):
        jj = jnp.minimum(j, nblk - 1)
        return jj // bps, jj % bps

    def prev(j):
        jj = jnp.maximum(j - 1, 0)
        return jj // bps, jj % bps

    return pl.pallas_call(
        functools.partial(_prompt_kernel, nchunks, nblk, bps),
        grid=(nblk + 1,),
        in_specs=[pl.BlockSpec((1, rb, d), lambda j: (*cur(j), 0)),
                  pl.BlockSpec((1, rb, d), lambda j: (*prev(j), 0))]
        + [_const_spec(c.shape) for c in consts],
        out_specs=[pl.BlockSpec((1, rb, d), lambda j: (*prev(j), 0)),
                   pl.BlockSpec((1, GLA_HEADS, GLA_DK, GLA_DV), lambda j: (cur(j)[0], 0, 0, 0))],
        out_shape=[jax.ShapeDtypeStruct((b, t, d), F32),
                   jax.ShapeDtypeStruct((b, GLA_HEADS, GLA_DK, GLA_DV), F32)],
        scratch_shapes=[pltpu.VMEM((rb, GLA_KW), F32), pltpu.VMEM((rb, GLA_KW), F32),
                        pltpu.VMEM((rb, GLA_KW), F32), pltpu.VMEM((rb, GLA_VW), BF16),
                        pltpu.VMEM((rb, GLA_VW), F32), pltpu.VMEM((2, rb, GLA_VW), BF16),
                        pltpu.VMEM((GLA_HEADS, GLA_DK, GLA_DV), F32)],
        compiler_params=pltpu.CompilerParams(dimension_semantics=("arbitrary",),
                                             vmem_limit_bytes=VMEM_LIMIT),
        name="prompt_trunk",
    )(x, x, *consts)


def _gla_sample_heads(mild, nseq, tlen, q_ref, k_ref, v_ref, gate_ref, cum_s, gon_ref, s_in_ref,
                      og_ref, s_out_ref, rxc, lower):
    r = nseq * tlen
    block_mask = (rxc < tlen) & lower
    seq_of_lane = lax.broadcasted_iota(jnp.int32, (GLA_DK, r), 1) >> 3
    for h in range(GLA_HEADS):
        ks = slice(h * GLA_DK, (h + 1) * GLA_DK)
        vs = slice(h * GLA_DV, (h + 1) * GLA_DV)
        q, k, cum = q_ref[:, ks], k_ref[:, ks], cum_s[:, ks]
        vb = v_ref[:, vs]
        qin = (q * jnp.exp(cum)).astype(BF16)
        if mild:
            scores = _intra_scores_direct(qin, k, cum, block_mask)
        else:
            scores = _intra_scores_factored(q, k, cum, (), rxc, lower)
        o_intra = _dot(scores.astype(BF16), vb)
        kd_t = (k * jnp.exp(_sublane_bcast(cum, tlen - 1, tlen) - cum)).T
        cum_t = cum.T
        for i in range(nseq):
            rs = slice(i * tlen, (i + 1) * tlen)
            state = s_in_ref[i, h]
            o = o_intra[rs] + _dot(qin[rs], state.astype(BF16))
            og_ref[rs, vs] = _out_norm_gate(o, gon_ref[...], gate_ref[rs, vs])
            last = cum_t[:, (i + 1) * tlen - 1:(i + 1) * tlen]
            kd_i = jnp.where(seq_of_lane == i, kd_t, 0.0).astype(BF16)
            s_out_ref[i, h] = jnp.exp(last) * state + _dot(kd_i, vb)


def _gla_sample_kernel(nseq, tlen, x_ref, g_ref, w_ref, wa_ref, ba_ref, gon_ref, s_in_ref,
                       og_ref, s_out_ref, q_s, k_s, v_s, gate_s, cum_s):
    hb = _rms(x_ref[...], g_ref[...]).astype(BF16)
    la = _log_decay(_dot(hb, w_ref[:, _A0:GLA_IN_PAD]), wa_ref, ba_ref)
    q_s[...] = _dot(hb, w_ref[:, _Q0:_K0]) * (GLA_DK ** -0.5)
    k_s[...] = _dot(hb, w_ref[:, _K0:_V0])
    rxc, lower = _index_masks(nseq * tlen)
    cum = _masked_prefix_sum(((rxc < tlen) & lower).astype(BF16), la)
    cum_s[...] = cum
    mild = jnp.min(cum) > -DECAY_GUARD
    v_s[...] = _dot(hb, w_ref[:, _V0:_G0]).astype(BF16)
    gate_s[...] = _silu(_dot(hb, w_ref[:, _G0:_A0]))
    args = (nseq, tlen, q_s, k_s, v_s, gate_s, cum_s, gon_ref, s_in_ref, og_ref, s_out_ref, rxc, lower)
    _gla_sample_heads(True, *args)
    pl.when(jnp.logical_not(mild))(functools.partial(_gla_sample_heads, False, *args))


def _gla_sample(x2d, p, state, nseq, tlen):
    n = x2d.shape[0]
    r = nseq * tlen
    assert tlen == SUB and r % CHUNK == 0
    row = lambda w: pl.BlockSpec((r, w), lambda i: (i, 0))
    st = pl.BlockSpec((nseq, GLA_HEADS, GLA_DK, GLA_DV), lambda i: (i, 0, 0, 0))
    consts = [p["g0"], p["w_in"], p["wa"], p["ba"], p["gon"]]
    return pl.pallas_call(
        functools.partial(_gla_sample_kernel, nseq, tlen),
        grid=(n // r,),
        in_specs=[row(D_MODEL)] + [_const_spec(c.shape) for c in consts] + [st],
        out_specs=[row(GLA_VW), st],
        out_shape=[jax.ShapeDtypeStruct((n, GLA_VW), BF16),
                   jax.ShapeDtypeStruct(state.shape, F32)],
        scratch_shapes=[pltpu.VMEM((r, GLA_KW), F32), pltpu.VMEM((r, GLA_KW), F32),
                        pltpu.VMEM((r, GLA_VW), BF16), pltpu.VMEM((r, GLA_VW), F32),
                        pltpu.VMEM((r, GLA_KW), F32)],
        compiler_params=pltpu.CompilerParams(dimension_semantics=("arbitrary",),
                                             vmem_limit_bytes=VMEM_LIMIT),
        name="gla_sample",
    )(x2d, *consts, state)


def _out_cmlp_kernel(nchunks, mix_block, x_ref, og_ref, wo_ref, g1_ref, wi_ref, lng_ref, lnb_ref, ws_ref,
                     bs_ref, wo2_ref, gf_ref, y_ref, v_out_ref):
    x1 = x_ref[...] + _dot(og_ref[...], wo_ref[:, 0:D_MODEL])
    y_ref[...] = _drain(_cmlp_steps(x1, nchunks, mix_block, g1_ref, wi_ref, lng_ref, lnb_ref, ws_ref, bs_ref,
                                    wo2_ref, gf_ref, v_out_ref))


def _out_cmlp(x2d, og, p, ws_eff, bs_full, mix_block, nchunks):
    n = x2d.shape[0]
    rb = nchunks * CHUNK
    row = lambda w: pl.BlockSpec((rb, w), lambda i: (i, 0))
    consts = [p["wo"], p["g1"], p["wi"], p["lng"], p["lnb"], ws_eff, bs_full, p["wo2"], p["gf"]]
    return pl.pallas_call(
        functools.partial(_out_cmlp_kernel, nchunks, mix_block),
        grid=(n // rb,),
        in_specs=[row(D_MODEL), row(GLA_VW)] + [_const_spec(c.shape) for c in consts],
        out_specs=[row(D_MODEL), row(CMLP_WIDTH)],
        out_shape=[jax.ShapeDtypeStruct((n, D_MODEL), F32),
                   jax.ShapeDtypeStruct((n, CMLP_WIDTH), F32)],
        compiler_params=pltpu.CompilerParams(dimension_semantics=("arbitrary",),
                                             vmem_limit_bytes=VMEM_LIMIT),
        name="out_cmlp",
    )(x2d, og, *consts)


def kernel(x_prompt, x_sample, state_gla, norm_g, gla_w_in, gla_w_a_up, gla_b_a_up, gla_g_onorm,
           gla_w_out, cmlp_w_in, cmlp_ln_g, cmlp_ln_b, cmlp_w_spatial, cmlp_b_spatial, cmlp_w_out,
           norm_final):
    b, t, d = x_prompt.shape
    nb, nt, _ = x_sample.shape
    assert t % (PROMPT_CHUNKS_PER_STEP * CHUNK) == 0 and (nb * nt) % CHUNK == 0 and CHUNK % nt == 0

    w_in = _stage_transposed(gla_w_in[0].T, GLA_IN_PAD)
    wo, wi, wo2 = _stage_weights(gla_w_out[0], cmlp_w_in[0], cmlp_w_out[0])
    p = {
        "g0": norm_g[0].reshape(1, d),
        "g1": norm_g[1].reshape(1, d),
        "w_in": w_in,
        "wa": jnp.pad(gla_w_a_up[0], ((0, LANE - GLA_GATE_RANK), (0, 0))).astype(BF16),
        "ba": gla_b_a_up[0].reshape(1, GLA_KW),
        "gon": gla_g_onorm[0].reshape(1, GLA_DV),
        "wo": wo,
        "wi": wi,
        "lng": cmlp_ln_g[0].reshape(1, CMLP_WIDTH),
        "lnb": cmlp_ln_b[0].reshape(1, CMLP_WIDTH),
        "wo2": wo2,
        "gf": norm_final.reshape(1, d),
    }
    ws = cmlp_w_spatial[0]
    bs = cmlp_b_spatial[0]

    def spatial_params(c):
        reps = CHUNK // c
        ws_eff = jnp.tile(ws[:, :c, :c], (1, reps, reps))
        bs_full = jnp.repeat(jnp.tile(bs[:, :c], (1, reps)).T, CMLP_GROUP_DIM, axis=1)
        return ws_eff, bs_full

    y_prompt, s_prompt = _prompt_trunk(x_prompt, p, *spatial_params(CHUNK), PROMPT_CHUNKS_PER_STEP)

    xs = x_sample.reshape(nb * nt, d)
    og, s_sample = _gla_sample(xs, p, state_gla[0], SAMPLE_SEQS_PER_STEP, nt)
    y_sample, v_rows = _out_cmlp(xs, og, p, *spatial_params(nt), nt, SAMPLE_CHUNKS_PER_STEP)

    return (y_prompt, y_sample.reshape(nb, nt, d), s_prompt[None], s_sample[None],
            v_rows.reshape(1, nb, nt, CMLP_WIDTH))
```

```python
import functools

import jax
import jax.numpy as jnp
from jax import lax
from jax.experimental import pallas as pl
from jax.experimental.pallas import tpu as pltpu

F32 = jnp.float32
BF16 = jnp.bfloat16

D_MODEL = 1024
GLA_HEADS = 4
GLA_DK = 128
GLA_DV = 256
GLA_KW = GLA_HEADS * GLA_DK
GLA_VW = GLA_HEADS * GLA_DV
GLA_GATE_RANK = 16
GLA_GATE_TAU = 16.0
CMLP_WIDTH = 1024
CMLP_GROUPS = 4
CMLP_GROUP_DIM = CMLP_WIDTH // CMLP_GROUPS
LANE = 128
CHUNK = 128
SUB = 8
PROMPT_CHUNKS_PER_STEP = 4
SAMPLE_CHUNKS_PER_STEP = 4
SAMPLE_SEQS_PER_STEP = 16
EPS = 1e-6
VMEM_LIMIT = 56 * 1024 * 1024
DECAY_GUARD = 60.0

_Q0, _K0, _V0, _G0, _A0 = 0, GLA_KW, 2 * GLA_KW, 2 * GLA_KW + GLA_VW, 2 * GLA_KW + 2 * GLA_VW
GLA_IN_PAD = _A0 + LANE


def _odd_tile_cols(cols):
    tiles = -(-cols // LANE)
    return (tiles + 1 - tiles % 2) * LANE


def _stage_weights_kernel(*refs):
    n = len(refs) // 2
    for src, dst in zip(refs[:n], refs[n:]):
        cols = src.shape[1]
        dst[:, 0:cols] = src[...].astype(BF16)
        if dst.shape[1] > cols:
            dst[:, cols:] = jnp.zeros((dst.shape[0], dst.shape[1] - cols), BF16)


def _stage_transposed_kernel(valid_rows, src, dst):
    i = pl.program_id(0)
    row = lax.broadcasted_iota(jnp.int32, src.shape, 0) + i * src.shape[0]
    dst[...] = jnp.where(row < valid_rows, src[...], 0.0).T.astype(BF16)


def _stage_transposed(w_t, cols):
    n, rows = w_t.shape
    blk = 5 * LANE
    assert cols % blk == 0 and cols >= n
    return pl.pallas_call(
        functools.partial(_stage_transposed_kernel, n),
        grid=(cols // blk,),
        in_specs=[pl.BlockSpec((blk, rows), lambda i: (i, 0))],
        out_specs=pl.BlockSpec((rows, blk), lambda i: (0, i)),
        out_shape=jax.ShapeDtypeStruct((rows, cols), BF16),
        compiler_params=pltpu.CompilerParams(dimension_semantics=("arbitrary",),
                                             vmem_limit_bytes=VMEM_LIMIT),
        name="stage_transposed",
    )(w_t)


def _stage_weights(*weights):
    rows = weights[0].shape[0]
    assert all(w.shape[0] == rows for w in weights)
    outs = [_odd_tile_cols(w.shape[1]) for w in weights]
    blk = lambda c: pl.BlockSpec((CHUNK, c), lambda i: (i, 0))
    return pl.pallas_call(
        _stage_weights_kernel,
        grid=(rows // CHUNK,),
        in_specs=[blk(w.shape[1]) for w in weights],
        out_specs=[blk(c) for c in outs],
        out_shape=[jax.ShapeDtypeStruct((rows, c), BF16) for c in outs],
        compiler_params=pltpu.CompilerParams(dimension_semantics=("arbitrary",),
                                             vmem_limit_bytes=VMEM_LIMIT),
        name="stage_weights",
    )(*weights)


def _dot(a, b):
    return jnp.dot(a, b, preferred_element_type=F32)


def _dot_nt(a, b):
    return lax.dot_general(a, b, (((1,), (1,)), ((), ())), preferred_element_type=F32)


def _rms(x, g):
    return x * lax.rsqrt(jnp.mean(x * x, axis=-1, keepdims=True) + EPS) * g


def _silu(x):
    return x * jax.nn.sigmoid(x)


def _masked_prefix_sum(mask01, x):
    hi = x.astype(BF16)
    mid = (x - hi.astype(F32)).astype(BF16)
    return _dot(jnp.concatenate([mask01, mask01], axis=1), jnp.concatenate([hi, mid], axis=0))


def _sublane_bcast(x, row, group):
    r, n = x.shape
    x3 = x.reshape(r // group, group, n)
    return jnp.broadcast_to(x3[:, row:row + 1, :], x3.shape).reshape(r, n)


def _index_masks(r):
    row = lax.broadcasted_iota(jnp.int32, (r, r), 0)
    col = lax.broadcasted_iota(jnp.int32, (r, r), 1)
    return row ^ col, col <= row


def _log_decay(a_low, wa_ref, ba_ref):
    z = _dot(a_low.astype(BF16), wa_ref[...]) + ba_ref[...]
    log_sig = jnp.minimum(z, 0.0) - jnp.log1p(jnp.exp(-jnp.abs(z)))
    return log_sig * (1.0 / GLA_GATE_TAU)


def _layernorm(v, g, b):
    vc = v - jnp.mean(v, axis=-1, keepdims=True)
    var = jnp.mean(vc * vc, axis=-1, keepdims=True)
    return vc * lax.rsqrt(var + EPS) * g + b


def _intra_scores_factored(q, k, cum, levels, rxc, lower):
    r = q.shape[0]
    parts = []
    for s in range(SUB):
        kb = _sublane_bcast(k, s, SUB)
        cb = _sublane_bcast(cum, s, SUB)
        parts.append((q * kb * jnp.exp(jnp.minimum(cum - cb, 0.0))).astype(BF16))
    acat = jnp.concatenate(parts, axis=1)
    pr = lax.broadcasted_iota(jnp.int32, (SUB * GLA_DK, r), 0)
    pc = lax.broadcasted_iota(jnp.int32, (SUB * GLA_DK, r), 1)
    lane_sum = ((pc & (SUB - 1)) == (pr >> 7)).astype(BF16)
    scores = jnp.where((rxc < SUB) & lower, _dot(acat, lane_sum), 0.0)
    for m in levels:
        qk = jnp.concatenate([(q if (i // m) % 2 else k)[i:i + m] for i in range(0, r, m)], axis=0)
        ref = _sublane_bcast(cum, m - 1, 2 * m)
        f = (qk * jnp.exp(-jnp.abs(cum - ref))).astype(BF16)
        scores = jnp.where((rxc >= m) & (rxc < 2 * m) & lower, _dot_nt(f, f), scores)
    return scores


def _intra_scores_direct(qin, k, cum, mask):
    kout = (k * jnp.exp(-cum)).astype(BF16)
    return jnp.where(mask, _dot_nt(qin, kout), 0.0)


def _out_norm_gate(o, gon, gate):
    return (_rms(o, gon) * gate).astype(BF16)


GLA_HEAD_STAGES = 3


def _gla_prompt_head(mild, h, rows, q_s, k_s, cum_s, v_s, gate_s, og_ref, s_in, s_out, gon_ref, rxc,
                     lower):
    ks = slice(h * GLA_DK, (h + 1) * GLA_DK)
    vs = slice(h * GLA_DV, (h + 1) * GLA_DV)
    q, k, cum = q_s[rows, ks], k_s[rows, ks], cum_s[rows, ks]
    vb = v_s[rows, vs]
    qin = (q * jnp.exp(cum)).astype(BF16)
    if mild:
        scores = _intra_scores_direct(qin, k, cum, lower)
    else:
        scores = _intra_scores_factored(q, k, cum, (64, 32, 16, 8), rxc, lower)
    k_t, cum_t = k.T, cum.T
    last = cum_t[:, CHUNK - 1:CHUNK]
    kd_t = (k_t * jnp.exp(last - cum_t)).astype(BF16)
    s_add = _dot(kd_t, vb)
    yield None
    state = s_in[h]
    o = _dot(jnp.concatenate([scores.astype(BF16), qin], axis=1),
             jnp.concatenate([vb, state.astype(BF16)], axis=0))
    s_out[h] = jnp.exp(last) * state + s_add
    yield None
    og_ref[rows, vs] = _out_norm_gate(o, gon_ref[...], gate_s[rows, vs])
    yield None


def _gla_prompt_chunk_stages(mild, c, *args):
    rows = slice(c * CHUNK, (c + 1) * CHUNK)
    gens = [_gla_prompt_head(mild, h, rows, *args) for h in range(GLA_HEADS)]
    for _ in range(GLA_HEAD_STAGES):
        for g in gens:
            next(g)
        yield None


def _interleave(first, n_first, second, n_second):
    last = None
    done_second = 0
    for i in range(n_first):
        next(first)
        while done_second * n_first < (i + 1) * n_second:
            last = next(second)
            done_second += 1
    return last


def _cmlp_steps(x1, nchunks, mix_block, g1_ref, wi_ref, lng_ref, lnb_ref, ws_ref, bs_ref, wo2_ref,
                gf_ref, v_out_ref=None):
    w = CMLP_WIDTH
    hb = _rms(x1, g1_ref[...]).astype(BF16)
    yield None
    u = _dot(hb, wi_ref[:, 0:w])
    yield None
    vn = _layernorm(_dot(hb, wi_ref[:, w:2 * w]), lng_ref[...], lnb_ref[...])
    if v_out_ref is not None:
        v_out_ref[...] = vn
    yield None
    ug = u * _silu(_dot(hb, wi_ref[:, 2 * w:3 * w]))
    yield None
    vnb = vn.astype(BF16)
    rxc, lower = _index_masks(CHUNK)
    mix_mask = (rxc < mix_block) & lower
    zs = []
    for g in range(CMLP_GROUPS):
        gs = slice(g * CMLP_GROUP_DIM, (g + 1) * CMLP_GROUP_DIM)
        wsm = jnp.where(mix_mask, ws_ref[g], 0.0).astype(BF16)
        mixed = jnp.concatenate(
            [_dot(wsm, vnb[c * CHUNK:(c + 1) * CHUNK, gs]) + bs_ref[:, gs] for c in range(nchunks)],
            axis=0)
        zs.append((ug[:, gs] * mixed).astype(BF16))
    yield None
    blocks = _cmlp_out_blocks(nchunks)
    rows = nchunks * CHUNK // blocks
    ys = []
    for b in range(blocks):
        rs = slice(b * rows, (b + 1) * rows)
        acc = x1[rs]
        for g in range(CMLP_GROUPS):
            gs = slice(g * CMLP_GROUP_DIM, (g + 1) * CMLP_GROUP_DIM)
            acc = acc + _dot(zs[g][rs], wo2_ref[gs, 0:D_MODEL])
            if (b, g) != (blocks - 1, CMLP_GROUPS - 1):
                yield None
        ys.append(_rms(acc, gf_ref[...]))
    yield jnp.concatenate(ys, axis=0)


def _cmlp_out_blocks(nchunks):
    return 2 if nchunks % 2 == 0 else 1


def _cmlp_num_steps(nchunks):
    return 5 + _cmlp_out_blocks(nchunks) * CMLP_GROUPS


def _drain(gen):
    out = None
    for out in gen:
        pass
    return out


def _prompt_kernel(nchunks, nblk, blk_per_seq, xa_ref, xc_ref, g0_ref, win_ref, wa_ref, ba_ref,
                   gon_ref, wo_ref, g1_ref, wi_ref, lng_ref, lnb_ref, ws_ref, bs_ref, wo2_ref, gf_ref,
                   y_ref, s_ref, q_s, k_s, cum_s, v_s, gate_s, og_s, st_s, mild_s):
    j = pl.program_id(0)
    has_gla = j < nblk
    is_first = jnp.logical_and(has_gla, lax.rem(j, blk_per_seq) == 0)
    slot = lax.rem(j, 2)
    og_cur, og_prev = og_s.at[slot], og_s.at[1 - slot]
    rxc, lower = _index_masks(CHUNK)

    @pl.when(is_first)
    def _():
        s_ref[...] = jnp.zeros_like(s_ref)

    def recurrence(is_mild, carried):
        for c in range(nchunks):
            yield from _gla_prompt_chunk_stages(
                is_mild, c, q_s, k_s, cum_s, v_s, gate_s, og_cur,
                s_ref.at[0] if c == 0 else carried, carried, gon_ref, rxc, lower)

    def step(with_gla, with_cmlp):
        cmlp = mild = hb = None
        if with_cmlp:
            x1 = xc_ref[0] + _dot(og_prev[...], wo_ref[:, 0:D_MODEL])
        if with_gla:
            hb = _rms(xa_ref[0], g0_ref[...]).astype(BF16)
            half = hb.shape[0] // 2
            a_low = jnp.concatenate([_dot(hb[:half], win_ref[:, _A0:GLA_IN_PAD]),
                                     _dot(hb[half:], win_ref[:, _A0:GLA_IN_PAD])], axis=0)
            la = _log_decay(a_low, wa_ref, ba_ref)
            q_s[...] = _dot(hb, win_ref[:, _Q0:_K0]) * (GLA_DK ** -0.5)
            k_s[...] = _dot(hb, win_ref[:, _K0:_V0])
            tri = lower.astype(BF16)
            total = None
            for c in range(nchunks):
                rows = slice(c * CHUNK, (c + 1) * CHUNK)
                cum = _masked_prefix_sum(tri, la[rows])
                cum_s[rows, :] = cum
                end = jnp.min(cum[CHUNK - SUB:CHUNK, :])
                total = end if total is None else jnp.minimum(total, end)
            mild = total > -DECAY_GUARD
            mild_s[0] = mild.astype(jnp.int32)
        if with_cmlp:
            cmlp = _cmlp_steps(x1, nchunks, CHUNK, g1_ref, wi_ref, lng_ref, lnb_ref, ws_ref, bs_ref,
                               wo2_ref, gf_ref)
            next(cmlp)
        if with_gla:
            v_s[...] = _dot(hb, win_ref[:, _V0:_G0]).astype(BF16)
            gate_s[...] = _silu(_dot(hb, win_ref[:, _G0:_A0]))
        if with_gla and with_cmlp:
            y_ref[0] = _interleave(recurrence(True, st_s), nchunks * GLA_HEAD_STAGES,
                                   cmlp, _cmlp_num_steps(nchunks) - 1)
        elif with_cmlp:
            y_ref[0] = _drain(cmlp)
        else:
            _drain(recurrence(True, st_s))
        if with_gla:
            for h in range(GLA_HEADS):
                s_ref[0, h] = jnp.where(mild, st_s[h], s_ref[0, h])

    pl.when(j == 0)(functools.partial(step, True, False))
    pl.when(jnp.logical_and(j > 0, has_gla))(functools.partial(step, True, True))
    pl.when(j == nblk)(functools.partial(step, False, True))

    @pl.when(jnp.logical_and(has_gla, mild_s[0] == 0))
    def _():
        _drain(recurrence(False, s_ref.at[0]))


def _const_spec(shape):
    nd = len(shape)
    return pl.BlockSpec(shape, lambda *_: (0,) * nd, pipeline_mode=pl.Buffered(1))


def _prompt_trunk(x, p, ws_eff, bs_full, nchunks):
    b, t, d = x.shape
    rb = nchunks * CHUNK
    bps = t // rb
    assert bps > 1
    nblk = b * bps
    consts = [p["g0"], p["w_in"], p["wa"], p["ba"], p["gon"], p["wo"], p["g1"], p["wi"], p["lng"],
              p["lnb"], ws_eff, bs_full, p["wo2"], p["gf"]]

    def cur(j):
        jj = jnp.minimum(j, nblk - 1)
        return jj // bps, jj % bps

    def prev(j):
        jj = jnp.maximum(j - 1, 0)
        return jj // bps, jj % bps

    return pl.pallas_call(
        functools.partial(_prompt_kernel, nchunks, nblk, bps),
        grid=(nblk + 1,),
        in_specs=[pl.BlockSpec((1, rb, d), lambda j: (*cur(j), 0)),
                  pl.BlockSpec((1, rb, d), lambda j: (*prev(j), 0))]
        + [_const_spec(c.shape) for c in consts],
        out_specs=[pl.BlockSpec((1, rb, d), lambda j: (*prev(j), 0)),
                   pl.BlockSpec((1, GLA_HEADS, GLA_DK, GLA_DV), lambda j: (cur(j)[0], 0, 0, 0))],
        out_shape=[jax.ShapeDtypeStruct((b, t, d), F32),
                   jax.ShapeDtypeStruct((b, GLA_HEADS, GLA_DK, GLA_DV), F32)],
        scratch_shapes=[pltpu.VMEM((rb, GLA_KW), F32), pltpu.VMEM((rb, GLA_KW), F32),
                        pltpu.VMEM((rb, GLA_KW), F32), pltpu.VMEM((rb, GLA_VW), BF16),
                        pltpu.VMEM((rb, GLA_VW), F32), pltpu.VMEM((2, rb, GLA_VW), BF16),
                        pltpu.VMEM((GLA_HEADS, GLA_DK, GLA_DV), F32), pltpu.SMEM((1,), jnp.int32)],
        compiler_params=pltpu.CompilerParams(dimension_semantics=("arbitrary",),
                                             vmem_limit_bytes=VMEM_LIMIT),
        name="prompt_trunk",
    )(x, x, *consts)


def _gla_sample_heads(mild, nseq, tlen, q_ref, k_ref, v_ref, gate_ref, cum_s, gon_ref, s_in_ref,
                      og_ref, s_out_ref, rxc, lower):
    r = nseq * tlen
    block_mask = (rxc < tlen) & lower
    seq_of_lane = lax.broadcasted_iota(jnp.int32, (GLA_DK, r), 1) >> 3
    for h in range(GLA_HEADS):
        ks = slice(h * GLA_DK, (h + 1) * GLA_DK)
        vs = slice(h * GLA_DV, (h + 1) * GLA_DV)
        q, k, cum = q_ref[:, ks], k_ref[:, ks], cum_s[:, ks]
        vb = v_ref[:, vs]
        qin = (q * jnp.exp(cum)).astype(BF16)
        if mild:
            scores = _intra_scores_direct(qin, k, cum, block_mask)
        else:
            scores = _intra_scores_factored(q, k, cum, (), rxc, lower)
        o_intra = _dot(scores.astype(BF16), vb)
        kd_t = (k * jnp.exp(_sublane_bcast(cum, tlen - 1, tlen) - cum)).T
        cum_t = cum.T
        for i in range(nseq):
            rs = slice(i * tlen, (i + 1) * tlen)
            state = s_in_ref[i, h]
            o = o_intra[rs] + _dot(qin[rs], state.astype(BF16))
            og_ref[rs, vs] = _out_norm_gate(o, gon_ref[...], gate_ref[rs, vs])
            last = cum_t[:, (i + 1) * tlen - 1:(i + 1) * tlen]
            kd_i = jnp.where(seq_of_lane == i, kd_t, 0.0).astype(BF16)
            s_out_ref[i, h] = jnp.exp(last) * state + _dot(kd_i, vb)


def _gla_sample_kernel(nseq, tlen, x_ref, g_ref, w_ref, wa_ref, ba_ref, gon_ref, s_in_ref,
                       og_ref, s_out_ref, q_s, k_s, v_s, gate_s, cum_s):
    hb = _rms(x_ref[...], g_ref[...]).astype(BF16)
    la = _log_decay(_dot(hb, w_ref[:, _A0:GLA_IN_PAD]), wa_ref, ba_ref)
    q_s[...] = _dot(hb, w_ref[:, _Q0:_K0]) * (GLA_DK ** -0.5)
    k_s[...] = _dot(hb, w_ref[:, _K0:_V0])
    rxc, lower = _index_masks(nseq * tlen)
    cum = _masked_prefix_sum(((rxc < tlen) & lower).astype(BF16), la)
    cum_s[...] = cum
    mild = jnp.min(cum) > -DECAY_GUARD
    v_s[...] = _dot(hb, w_ref[:, _V0:_G0]).astype(BF16)
    gate_s[...] = _silu(_dot(hb, w_ref[:, _G0:_A0]))
    args = (nseq, tlen, q_s, k_s, v_s, gate_s, cum_s, gon_ref, s_in_ref, og_ref, s_out_ref, rxc, lower)
    _gla_sample_heads(True, *args)
    pl.when(jnp.logical_not(mild))(functools.partial(_gla_sample_heads, False, *args))


def _gla_sample(x2d, p, state, nseq, tlen):
    n = x2d.shape[0]
    r = nseq * tlen
    assert tlen == SUB and r % CHUNK == 0
    row = lambda w: pl.BlockSpec((r, w), lambda i: (i, 0))
    st = pl.BlockSpec((nseq, GLA_HEADS, GLA_DK, GLA_DV), lambda i: (i, 0, 0, 0))
    consts = [p["g0"], p["w_in"], p["wa"], p["ba"], p["gon"]]
    return pl.pallas_call(
        functools.partial(_gla_sample_kernel, nseq, tlen),
        grid=(n // r,),
        in_specs=[row(D_MODEL)] + [_const_spec(c.shape) for c in consts] + [st],
        out_specs=[row(GLA_VW), st],
        out_shape=[jax.ShapeDtypeStruct((n, GLA_VW), BF16),
                   jax.ShapeDtypeStruct(state.shape, F32)],
        scratch_shapes=[pltpu.VMEM((r, GLA_KW), F32), pltpu.VMEM((r, GLA_KW), F32),
                        pltpu.VMEM((r, GLA_VW), BF16), pltpu.VMEM((r, GLA_VW), F32),
                        pltpu.VMEM((r, GLA_KW), F32)],
        compiler_params=pltpu.CompilerParams(dimension_semantics=("arbitrary",),
                                             vmem_limit_bytes=VMEM_LIMIT),
        name="gla_sample",
    )(x2d, *consts, state)


def _out_cmlp_kernel(nchunks, mix_block, x_ref, og_ref, wo_ref, g1_ref, wi_ref, lng_ref, lnb_ref, ws_ref,
                     bs_ref, wo2_ref, gf_ref, y_ref, v_out_ref):
    x1 = x_ref[...] + _dot(og_ref[...], wo_ref[:, 0:D_MODEL])
    y_ref[...] = _drain(_cmlp_steps(x1, nchunks, mix_block, g1_ref, wi_ref, lng_ref, lnb_ref, ws_ref, bs_ref,
                                    wo2_ref, gf_ref, v_out_ref))


def _out_cmlp(x2d, og, p, ws_eff, bs_full, mix_block, nchunks):
    n = x2d.shape[0]
    rb = nchunks * CHUNK
    row = lambda w: pl.BlockSpec((rb, w), lambda i: (i, 0))
    consts = [p["wo"], p["g1"], p["wi"], p["lng"], p["lnb"], ws_eff, bs_full, p["wo2"], p["gf"]]
    return pl.pallas_call(
        functools.partial(_out_cmlp_kernel, nchunks, mix_block),
        grid=(n // rb,),
        in_specs=[row(D_MODEL), row(GLA_VW)] + [_const_spec(c.shape) for c in consts],
        out_specs=[row(D_MODEL), row(CMLP_WIDTH)],
        out_shape=[jax.ShapeDtypeStruct((n, D_MODEL), F32),
                   jax.ShapeDtypeStruct((n, CMLP_WIDTH), F32)],
        compiler_params=pltpu.CompilerParams(dimension_semantics=("arbitrary",),
                                             vmem_limit_bytes=VMEM_LIMIT),
        name="out_cmlp",
    )(x2d, og, *consts)


def kernel(x_prompt, x_sample, state_gla, norm_g, gla_w_in, gla_w_a_up, gla_b_a_up, gla_g_onorm,
           gla_w_out, cmlp_w_in, cmlp_ln_g, cmlp_ln_b, cmlp_w_spatial, cmlp_b_spatial, cmlp_w_out,
           norm_final):
    b, t, d = x_prompt.shape
    nb, nt, _ = x_sample.shape
    assert t % (PROMPT_CHUNKS_PER_STEP * CHUNK) == 0 and (nb * nt) % CHUNK == 0 and CHUNK % nt == 0

    w_in = _stage_transposed(gla_w_in[0].T, GLA_IN_PAD)
    wo, wi, wo2 = _stage_weights(gla_w_out[0], cmlp_w_in[0], cmlp_w_out[0])
    p = {
        "g0": norm_g[0].reshape(1, d),
        "g1": norm_g[1].reshape(1, d),
        "w_in": w_in,
        "wa": jnp.pad(gla_w_a_up[0], ((0, LANE - GLA_GATE_RANK), (0, 0))).astype(BF16),
        "ba": gla_b_a_up[0].reshape(1, GLA_KW),
        "gon": gla_g_onorm[0].reshape(1, GLA_DV),
        "wo": wo,
        "wi": wi,
        "lng": cmlp_ln_g[0].reshape(1, CMLP_WIDTH),
        "lnb": cmlp_ln_b[0].reshape(1, CMLP_WIDTH),
        "wo2": wo2,
        "gf": norm_final.reshape(1, d),
    }
    ws = cmlp_w_spatial[0]
    bs = cmlp_b_spatial[0]

    def spatial_params(c):
        reps = CHUNK // c
        ws_eff = jnp.tile(ws[:, :c, :c], (1, reps, reps))
        bs_full = jnp.repeat(jnp.tile(bs[:, :c], (1, reps)).T, CMLP_GROUP_DIM, axis=1)
        return ws_eff, bs_full

    y_prompt, s_prompt = _prompt_trunk(x_prompt, p, *spatial_params(CHUNK), PROMPT_CHUNKS_PER_STEP)

    xs = x_sample.reshape(nb * nt, d)
    og, s_sample = _gla_sample(xs, p, state_gla[0], SAMPLE_SEQS_PER_STEP, nt)
    y_sample, v_rows = _out_cmlp(xs, og, p, *spatial_params(nt), nt, SAMPLE_CHUNKS_PER_STEP)

    return (y_prompt, y_sample.reshape(nb, nt, d), s_prompt[None], s_sample[None],
            v_rows.reshape(1, nb, nt, CMLP_WIDTH))
```

```python
import functools

import jax
import jax.numpy as jnp
from jax import lax
from jax.experimental import pallas as pl
from jax.experimental.pallas import tpu as pltpu

F32 = jnp.float32
BF16 = jnp.bfloat16

D_MODEL = 1024
GLA_HEADS = 4
GLA_DK = 128
GLA_DV = 256
GLA_KW = GLA_HEADS * GLA_DK
GLA_VW = GLA_HEADS * GLA_DV
GLA_GATE_RANK = 16
GLA_GATE_TAU = 16.0
CMLP_WIDTH = 1024
CMLP_GROUPS = 4
CMLP_GROUP_DIM = CMLP_WIDTH // CMLP_GROUPS
LANE = 128
CHUNK = 128
SUB = 8
PROMPT_CHUNKS_PER_STEP = 4
SAMPLE_CHUNKS_PER_STEP = 4
SAMPLE_SEQS_PER_STEP = 16
EPS = 1e-6
VMEM_LIMIT = 56 * 1024 * 1024
DECAY_GUARD = 60.0

_Q0, _K0, _V0, _G0, _A0 = 0, GLA_KW, 2 * GLA_KW, 2 * GLA_KW + GLA_VW, 2 * GLA_KW + 2 * GLA_VW
GLA_IN_PAD = _A0 + LANE


def _odd_tile_cols(cols):
    tiles = -(-cols // LANE)
    return (tiles + 1 - tiles % 2) * LANE


def _stage_weights_kernel(*refs):
    n = len(refs) // 2
    for src, dst in zip(refs[:n], refs[n:]):
        cols = src.shape[1]
        dst[:, 0:cols] = src[...].astype(BF16)
        if dst.shape[1] > cols:
            dst[:, cols:] = jnp.zeros((dst.shape[0], dst.shape[1] - cols), BF16)


def _stage_transposed_kernel(valid_rows, src, dst):
    i = pl.program_id(0)
    row = lax.broadcasted_iota(jnp.int32, src.shape, 0) + i * src.shape[0]
    dst[...] = jnp.where(row < valid_rows, src[...], 0.0).T.astype(BF16)


def _stage_transposed(w_t, cols):
    n, rows = w_t.shape
    blk = 5 * LANE
    assert cols % blk == 0 and cols >= n
    return pl.pallas_call(
        functools.partial(_stage_transposed_kernel, n),
        grid=(cols // blk,),
        in_specs=[pl.BlockSpec((blk, rows), lambda i: (i, 0))],
        out_specs=pl.BlockSpec((rows, blk), lambda i: (0, i)),
        out_shape=jax.ShapeDtypeStruct((rows, cols), BF16),
        compiler_params=pltpu.CompilerParams(dimension_semantics=("arbitrary",),
                                             vmem_limit_bytes=VMEM_LIMIT),
        name="stage_transposed",
    )(w_t)


def _stage_weights(*weights):
    rows = weights[0].shape[0]
    assert all(w.shape[0] == rows for w in weights)
    outs = [_odd_tile_cols(w.shape[1]) for w in weights]
    blk = lambda c: pl.BlockSpec((CHUNK, c), lambda i: (i, 0))
    return pl.pallas_call(
        _stage_weights_kernel,
        grid=(rows // CHUNK,),
        in_specs=[blk(w.shape[1]) for w in weights],
        out_specs=[blk(c) for c in outs],
        out_shape=[jax.ShapeDtypeStruct((rows, c), BF16) for c in outs],
        compiler_params=pltpu.CompilerParams(dimension_semantics=("arbitrary",),
                                             vmem_limit_bytes=VMEM_LIMIT),
        name="stage_weights",
    )(*weights)


def _dot(a, b):
    return jnp.dot(a, b, preferred_element_type=F32)


def _dot_nt(a, b):
    return lax.dot_general(a, b, (((1,), (1,)), ((), ())), preferred_element_type=F32)


def _rms(x, g):
    return x * lax.rsqrt(jnp.mean(x * x, axis=-1, keepdims=True) + EPS) * g


def _silu(x):
    return x * jax.nn.sigmoid(x)


def _masked_prefix_sum(mask01, x):
    hi = x.astype(BF16)
    mid = (x - hi.astype(F32)).astype(BF16)
    return _dot(jnp.concatenate([mask01, mask01], axis=1), jnp.concatenate([hi, mid], axis=0))


def _sublane_bcast(x, row, group):
    r, n = x.shape
    x3 = x.reshape(r // group, group, n)
    return jnp.broadcast_to(x3[:, row:row + 1, :], x3.shape).reshape(r, n)


def _index_masks(r):
    row = lax.broadcasted_iota(jnp.int32, (r, r), 0)
    col = lax.broadcasted_iota(jnp.int32, (r, r), 1)
    return row ^ col, col <= row


def _log_decay(a_low, wa_ref, ba_ref):
    z = _dot(a_low.astype(BF16), wa_ref[...]) + ba_ref[...]
    log_sig = jnp.minimum(z, 0.0) - jnp.log1p(jnp.exp(-jnp.abs(z)))
    return log_sig * (1.0 / GLA_GATE_TAU)


def _layernorm(v, g, b):
    vc = v - jnp.mean(v, axis=-1, keepdims=True)
    var = jnp.mean(vc * vc, axis=-1, keepdims=True)
    return vc * lax.rsqrt(var + EPS) * g + b


def _intra_scores_factored(q, k, cum, levels, rxc, lower):
    r = q.shape[0]
    parts = []
    for s in range(SUB):
        kb = _sublane_bcast(k, s, SUB)
        cb = _sublane_bcast(cum, s, SUB)
        parts.append((q * kb * jnp.exp(jnp.minimum(cum - cb, 0.0))).astype(BF16))
    acat = jnp.concatenate(parts, axis=1)
    pr = lax.broadcasted_iota(jnp.int32, (SUB * GLA_DK, r), 0)
    pc = lax.broadcasted_iota(jnp.int32, (SUB * GLA_DK, r), 1)
    lane_sum = ((pc & (SUB - 1)) == (pr >> 7)).astype(BF16)
    scores = jnp.where((rxc < SUB) & lower, _dot(acat, lane_sum), 0.0)
    for m in levels:
        qk = jnp.concatenate([(q if (i // m) % 2 else k)[i:i + m] for i in range(0, r, m)], axis=0)
        ref = _sublane_bcast(cum, m - 1, 2 * m)
        f = (qk * jnp.exp(-jnp.abs(cum - ref))).astype(BF16)
        scores = jnp.where((rxc >= m) & (rxc < 2 * m) & lower, _dot_nt(f, f), scores)
    return scores


def _intra_scores_direct(qin, k, cum, mask):
    kout = (k * jnp.exp(-cum)).astype(BF16)
    return jnp.where(mask, _dot_nt(qin, kout), 0.0)


def _out_norm_gate(o, gon, gate):
    return (_rms(o, gon) * gate).astype(BF16)


GLA_HEAD_STAGES = 3


def _gla_prompt_head(mild, h, rows, q_s, k_s, cum_s, v_s, gate_s, og_ref, s_in, s_out, gon_ref, rxc,
                     lower):
    ks = slice(h * GLA_DK, (h + 1) * GLA_DK)
    vs = slice(h * GLA_DV, (h + 1) * GLA_DV)
    q, k, cum = q_s[rows, ks], k_s[rows, ks], cum_s[rows, ks]
    vb = v_s[rows, vs]
    qin = (q * jnp.exp(cum)).astype(BF16)
    if mild:
        scores = _intra_scores_direct(qin, k, cum, lower)
    else:
        scores = _intra_scores_factored(q, k, cum, (64, 32, 16, 8), rxc, lower)
    k_t, cum_t = k.T, cum.T
    last = cum_t[:, CHUNK - 1:CHUNK]
    kd_t = (k_t * jnp.exp(last - cum_t)).astype(BF16)
    s_add = _dot(kd_t, vb)
    yield None
    state = s_in[h]
    o = _dot(jnp.concatenate([scores.astype(BF16), qin], axis=1),
             jnp.concatenate([vb, state.astype(BF16)], axis=0))
    s_out[h] = jnp.exp(last) * state + s_add
    yield None
    og_ref[rows, vs] = _out_norm_gate(o, gon_ref[...], gate_s[rows, vs])
    yield None


def _gla_prompt_chunk_stages(mild, c, *args):
    rows = slice(c * CHUNK, (c + 1) * CHUNK)
    gens = [_gla_prompt_head(mild, h, rows, *args) for h in range(GLA_HEADS)]
    for _ in range(GLA_HEAD_STAGES):
        for g in gens:
            next(g)
        yield None


def _interleave(first, n_first, second, n_second):
    last = None
    done_second = 0
    for i in range(n_first):
        next(first)
        while done_second * n_first < (i + 1) * n_second:
            last = next(second)
            done_second += 1
    return last


def _cmlp_steps(x1, nchunks, mix_block, g1_ref, wi_ref, lng_ref, lnb_ref, ws_ref, bs_ref, wo2_ref,
                gf_ref, v_out_ref=None):
    w = CMLP_WIDTH
    hb = _rms(x1, g1_ref[...]).astype(BF16)
    yield None
    u = _dot(hb, wi_ref[:, 0:w])
    yield None
    vn = _layernorm(_dot(hb, wi_ref[:, w:2 * w]), lng_ref[...], lnb_ref[...])
    if v_out_ref is not None:
        v_out_ref[...] = vn
    yield None
    ug = u * _silu(_dot(hb, wi_ref[:, 2 * w:3 * w]))
    yield None
    vnb = vn.astype(BF16)
    rxc, lower = _index_masks(CHUNK)
    mix_mask = (rxc < mix_block) & lower
    zs = []
    for g in range(CMLP_GROUPS):
        gs = slice(g * CMLP_GROUP_DIM, (g + 1) * CMLP_GROUP_DIM)
        wsm = jnp.where(mix_mask, ws_ref[g], 0.0).astype(BF16)
        mixed = jnp.concatenate(
            [_dot(wsm, vnb[c * CHUNK:(c + 1) * CHUNK, gs]) + bs_ref[:, gs] for c in range(nchunks)],
            axis=0)
        zs.append((ug[:, gs] * mixed).astype(BF16))
    yield None
    blocks = _cmlp_out_blocks(nchunks)
    rows = nchunks * CHUNK // blocks
    ys = []
    for b in range(blocks):
        rs = slice(b * rows, (b + 1) * rows)
        acc = x1[rs]
        for g in range(CMLP_GROUPS):
            gs = slice(g * CMLP_GROUP_DIM, (g + 1) * CMLP_GROUP_DIM)
            acc = acc + _dot(zs[g][rs], wo2_ref[gs, 0:D_MODEL])
            if (b, g) != (blocks - 1, CMLP_GROUPS - 1):
                yield None
        ys.append(_rms(acc, gf_ref[...]))
    yield jnp.concatenate(ys, axis=0)


def _cmlp_out_blocks(nchunks):
    return 2 if nchunks % 2 == 0 else 1


def _cmlp_num_steps(nchunks):
    return 5 + _cmlp_out_blocks(nchunks) * CMLP_GROUPS


def _drain(gen):
    out = None
    for out in gen:
        pass
    return out


def _prompt_kernel(nchunks, nblk, blk_per_seq, xa_ref, xc_ref, g0_ref, win_ref, wa_ref, ba_ref,
                   gon_ref, wo_ref, g1_ref, wi_ref, lng_ref, lnb_ref, ws_ref, bs_ref, wo2_ref, gf_ref,
                   y_ref, s_ref, q_s, k_s, cum_s, v_s, gate_s, og_s, st_s, mild_s):
    j = pl.program_id(0)
    has_gla = j < nblk
    is_first = jnp.logical_and(has_gla, lax.rem(j, blk_per_seq) == 0)
    slot = lax.rem(j, 2)
    og_cur, og_prev = og_s.at[slot], og_s.at[1 - slot]
    rxc, lower = _index_masks(CHUNK)

    @pl.when(is_first)
    def _():
        s_ref[...] = jnp.zeros_like(s_ref)

    def recurrence(is_mild, carried):
        for c in range(nchunks):
            yield from _gla_prompt_chunk_stages(
                is_mild, c, q_s, k_s, cum_s, v_s, gate_s, og_cur,
                s_ref.at[0] if c == 0 else carried, carried, gon_ref, rxc, lower)

    def step(with_gla, with_cmlp):
        cmlp = mild = hb = None
        if with_cmlp:
            x1 = xc_ref[0] + _dot(og_prev[...], wo_ref[:, 0:D_MODEL])
        if with_gla:
            hb = _rms(xa_ref[0], g0_ref[...]).astype(BF16)
            half = hb.shape[0] // 2
            a_low = jnp.concatenate([_dot(hb[:half], win_ref[:, _A0:GLA_IN_PAD]),
                                     _dot(hb[half:], win_ref[:, _A0:GLA_IN_PAD])], axis=0)
            la = _log_decay(a_low, wa_ref, ba_ref)
            q_s[...] = _dot(hb, win_ref[:, _Q0:_K0]) * (GLA_DK ** -0.5)
            k_s[...] = _dot(hb, win_ref[:, _K0:_V0])
            tri = lower.astype(BF16)
            total = None
            for c in range(nchunks):
                rows = slice(c * CHUNK, (c + 1) * CHUNK)
                cum = _masked_prefix_sum(tri, la[rows])
                cum_s[rows, :] = cum
                end = jnp.min(cum[CHUNK - SUB:CHUNK, :])
                total = end if total is None else jnp.minimum(total, end)
            mild = total > -DECAY_GUARD
            mild_s[0] = mild.astype(jnp.int32)
        if with_cmlp:
            cmlp = _cmlp_steps(x1, nchunks, CHUNK, g1_ref, wi_ref, lng_ref, lnb_ref, ws_ref, bs_ref,
                               wo2_ref, gf_ref)
            next(cmlp)
        if with_gla:
            v_s[...] = _dot(hb, win_ref[:, _V0:_G0]).astype(BF16)
            gate_s[...] = _silu(_dot(hb, win_ref[:, _G0:_A0]))
        if with_gla and with_cmlp:
            y_ref[0] = _interleave(recurrence(True, st_s), nchunks * GLA_HEAD_STAGES,
                                   cmlp, _cmlp_num_steps(nchunks) - 1)
        elif with_cmlp:
            y_ref[0] = _drain(cmlp)
        else:
            _drain(recurrence(True, st_s))
        if with_gla:
            for h in range(GLA_HEADS):
                s_ref[0, h] = jnp.where(mild, st_s[h], s_ref[0, h])

    pl.when(j == 0)(functools.partial(step, True, False))
    pl.when(jnp.logical_and(j > 0, has_gla))(functools.partial(step, True, True))
    pl.when(j == nblk)(functools.partial(step, False, True))

    @pl.when(jnp.logical_and(has_gla, mild_s[0] == 0))
    def _():
        def redo_chunk(c, carry):
            rows = pl.ds(pl.multiple_of(c * CHUNK, CHUNK), CHUNK)
            for h in range(GLA_HEADS):
                _drain(_gla_prompt_head(False, h, rows, q_s, k_s, cum_s, v_s, gate_s, og_cur,
                                        s_ref.at[0], s_ref.at[0], gon_ref, rxc, lower))
            return carry

        lax.fori_loop(0, nchunks, redo_chunk, 0)


def _const_spec(shape):
    nd = len(shape)
    return pl.BlockSpec(shape, lambda *_: (0,) * nd, pipeline_mode=pl.Buffered(1))


def _prompt_trunk(x, p, ws_eff, bs_full, nchunks):
    b, t, d = x.shape
    rb = nchunks * CHUNK
    bps = t // rb
    assert bps > 1
    nblk = b * bps
    consts = [p["g0"], p["w_in"], p["wa"], p["ba"], p["gon"], p["wo"], p["g1"], p["wi"], p["lng"],
              p["lnb"], ws_eff, bs_full, p["wo2"], p["gf"]]

    def cur(j):
        jj = jnp.minimum(j, nblk - 1)
        return jj // bps, jj % bps

    def prev(j):
        jj = jnp.maximum(j - 1, 0)
        return jj // bps, jj % bps

    return pl.pallas_call(
        functools.partial(_prompt_kernel, nchunks, nblk, bps),
        grid=(nblk + 1,),
        in_specs=[pl.BlockSpec((1, rb, d), lambda j: (*cur(j), 0)),
                  pl.BlockSpec((1, rb, d), lambda j: (*prev(j), 0))]
        + [_const_spec(c.shape) for c in consts],
        out_specs=[pl.BlockSpec((1, rb, d), lambda j: (*prev(j), 0)),
                   pl.BlockSpec((1, GLA_HEADS, GLA_DK, GLA_DV), lambda j: (cur(j)[0], 0, 0, 0))],
        out_shape=[jax.ShapeDtypeStruct((b, t, d), F32),
                   jax.ShapeDtypeStruct((b, GLA_HEADS, GLA_DK, GLA_DV), F32)],
        scratch_shapes=[pltpu.VMEM((rb, GLA_KW), F32), pltpu.VMEM((rb, GLA_KW), F32),
                        pltpu.VMEM((rb, GLA_KW), F32), pltpu.VMEM((rb, GLA_VW), BF16),
                        pltpu.VMEM((rb, GLA_VW), F32), pltpu.VMEM((2, rb, GLA_VW), BF16),
                        pltpu.VMEM((GLA_HEADS, GLA_DK, GLA_DV), F32), pltpu.SMEM((1,), jnp.int32)],
        compiler_params=pltpu.CompilerParams(dimension_semantics=("arbitrary",),
                                             vmem_limit_bytes=VMEM_LIMIT),
        name="prompt_trunk",
    )(x, x, *consts)


def _gla_sample_heads(mild, nseq, tlen, q_ref, k_ref, v_ref, gate_ref, cum_s, gon_ref, s_in_ref,
                      og_ref, s_out_ref, rxc, lower):
    r = nseq * tlen
    block_mask = (rxc < tlen) & lower
    seq_of_lane = lax.broadcasted_iota(jnp.int32, (GLA_DK, r), 1) >> 3
    for h in range(GLA_HEADS):
        ks = slice(h * GLA_DK, (h + 1) * GLA_DK)
        vs = slice(h * GLA_DV, (h + 1) * GLA_DV)
        q, k, cum = q_ref[:, ks], k_ref[:, ks], cum_s[:, ks]
        vb = v_ref[:, vs]
        qin = (q * jnp.exp(cum)).astype(BF16)
        if mild:
            scores = _intra_scores_direct(qin, k, cum, block_mask)
        else:
            scores = _intra_scores_factored(q, k, cum, (), rxc, lower)
        o_intra = _dot(scores.astype(BF16), vb)
        kd_t = (k * jnp.exp(_sublane_bcast(cum, tlen - 1, tlen) - cum)).T
        cum_t = cum.T
        for i in range(nseq):
            rs = slice(i * tlen, (i + 1) * tlen)
            state = s_in_ref[i, h]
            o = o_intra[rs] + _dot(qin[rs], state.astype(BF16))
            og_ref[rs, vs] = _out_norm_gate(o, gon_ref[...], gate_ref[rs, vs])
            last = cum_t[:, (i + 1) * tlen - 1:(i + 1) * tlen]
            kd_i = jnp.where(seq_of_lane == i, kd_t, 0.0).astype(BF16)
            s_out_ref[i, h] = jnp.exp(last) * state + _dot(kd_i, vb)


def _gla_sample_kernel(nseq, tlen, x_ref, g_ref, w_ref, wa_ref, ba_ref, gon_ref, s_in_ref,
                       og_ref, s_out_ref, q_s, k_s, v_s, gate_s, cum_s):
    hb = _rms(x_ref[...], g_ref[...]).astype(BF16)
    la = _log_decay(_dot(hb, w_ref[:, _A0:GLA_IN_PAD]), wa_ref, ba_ref)
    q_s[...] = _dot(hb, w_ref[:, _Q0:_K0]) * (GLA_DK ** -0.5)
    k_s[...] = _dot(hb, w_ref[:, _K0:_V0])
    rxc, lower = _index_masks(nseq * tlen)
    cum = _masked_prefix_sum(((rxc < tlen) & lower).astype(BF16), la)
    cum_s[...] = cum
    mild = jnp.min(cum) > -DECAY_GUARD
    v_s[...] = _dot(hb, w_ref[:, _V0:_G0]).astype(BF16)
    gate_s[...] = _silu(_dot(hb, w_ref[:, _G0:_A0]))
    args = (nseq, tlen, q_s, k_s, v_s, gate_s, cum_s, gon_ref, s_in_ref, og_ref, s_out_ref, rxc, lower)
    _gla_sample_heads(True, *args)
    pl.when(jnp.logical_not(mild))(functools.partial(_gla_sample_heads, False, *args))


def _gla_sample(x2d, p, state, nseq, tlen):
    n = x2d.shape[0]
    r = nseq * tlen
    assert tlen == SUB and r % CHUNK == 0
    row = lambda w: pl.BlockSpec((r, w), lambda i: (i, 0))
    st = pl.BlockSpec((nseq, GLA_HEADS, GLA_DK, GLA_DV), lambda i: (i, 0, 0, 0))
    consts = [p["g0"], p["w_in"], p["wa"], p["ba"], p["gon"]]
    return pl.pallas_call(
        functools.partial(_gla_sample_kernel, nseq, tlen),
        grid=(n // r,),
        in_specs=[row(D_MODEL)] + [_const_spec(c.shape) for c in consts] + [st],
        out_specs=[row(GLA_VW), st],
        out_shape=[jax.ShapeDtypeStruct((n, GLA_VW), BF16),
                   jax.ShapeDtypeStruct(state.shape, F32)],
        scratch_shapes=[pltpu.VMEM((r, GLA_KW), F32), pltpu.VMEM((r, GLA_KW), F32),
                        pltpu.VMEM((r, GLA_VW), BF16), pltpu.VMEM((r, GLA_VW), F32),
                        pltpu.VMEM((r, GLA_KW), F32)],
        compiler_params=pltpu.CompilerParams(dimension_semantics=("arbitrary",),
                                             vmem_limit_bytes=VMEM_LIMIT),
        name="gla_sample",
    )(x2d, *consts, state)


def _out_cmlp_kernel(nchunks, mix_block, x_ref, og_ref, wo_ref, g1_ref, wi_ref, lng_ref, lnb_ref, ws_ref,
                     bs_ref, wo2_ref, gf_ref, y_ref, v_out_ref):
    x1 = x_ref[...] + _dot(og_ref[...], wo_ref[:, 0:D_MODEL])
    y_ref[...] = _drain(_cmlp_steps(x1, nchunks, mix_block, g1_ref, wi_ref, lng_ref, lnb_ref, ws_ref, bs_ref,
                                    wo2_ref, gf_ref, v_out_ref))


def _out_cmlp(x2d, og, p, ws_eff, bs_full, mix_block, nchunks):
    n = x2d.shape[0]
    rb = nchunks * CHUNK
    row = lambda w: pl.BlockSpec((rb, w), lambda i: (i, 0))
    consts = [p["wo"], p["g1"], p["wi"], p["lng"], p["lnb"], ws_eff, bs_full, p["wo2"], p["gf"]]
    return pl.pallas_call(
        functools.partial(_out_cmlp_kernel, nchunks, mix_block),
        grid=(n // rb,),
        in_specs=[row(D_MODEL), row(GLA_VW)] + [_const_spec(c.shape) for c in consts],
        out_specs=[row(D_MODEL), row(CMLP_WIDTH)],
        out_shape=[jax.ShapeDtypeStruct((n, D_MODEL), F32),
                   jax.ShapeDtypeStruct((n, CMLP_WIDTH), F32)],
        compiler_params=pltpu.CompilerParams(dimension_semantics=("arbitrary",),
                                             vmem_limit_bytes=VMEM_LIMIT),
        name="out_cmlp",
    )(x2d, og, *consts)


def kernel(x_prompt, x_sample, state_gla, norm_g, gla_w_in, gla_w_a_up, gla_b_a_up, gla_g_onorm,
           gla_w_out, cmlp_w_in, cmlp_ln_g, cmlp_ln_b, cmlp_w_spatial, cmlp_b_spatial, cmlp_w_out,
           norm_final):
    b, t, d = x_prompt.shape
    nb, nt, _ = x_sample.shape
    assert t % (PROMPT_CHUNKS_PER_STEP * CHUNK) == 0 and (nb * nt) % CHUNK == 0 and CHUNK % nt == 0

    w_in = _stage_transposed(gla_w_in[0].T, GLA_IN_PAD)
    wo, wi, wo2 = _stage_weights(gla_w_out[0], cmlp_w_in[0], cmlp_w_out[0])
    p = {
        "g0": norm_g[0].reshape(1, d),
        "g1": norm_g[1].reshape(1, d),
        "w_in": w_in,
        "wa": jnp.pad(gla_w_a_up[0], ((0, LANE - GLA_GATE_RANK), (0, 0))).astype(BF16),
        "ba": gla_b_a_up[0].reshape(1, GLA_KW),
        "gon": gla_g_onorm[0].reshape(1, GLA_DV),
        "wo": wo,
        "wi": wi,
        "lng": cmlp_ln_g[0].reshape(1, CMLP_WIDTH),
        "lnb": cmlp_ln_b[0].reshape(1, CMLP_WIDTH),
        "wo2": wo2,
        "gf": norm_final.reshape(1, d),
    }
    ws = cmlp_w_spatial[0]
    bs = cmlp_b_spatial[0]

    def spatial_params(c):
        reps = CHUNK // c
        ws_eff = jnp.tile(ws[:, :c, :c], (1, reps, reps))
        bs_full = jnp.repeat(jnp.tile(bs[:, :c], (1, reps)).T, CMLP_GROUP_DIM, axis=1)
        return ws_eff, bs_full

    y_prompt, s_prompt = _prompt_trunk(x_prompt, p, *spatial_params(CHUNK), PROMPT_CHUNKS_PER_STEP)

    xs = x_sample.reshape(nb * nt, d)
    og, s_sample = _gla_sample(xs, p, state_gla[0], SAMPLE_SEQS_PER_STEP, nt)
    y_sample, v_rows = _out_cmlp(xs, og, p, *spatial_params(nt), nt, SAMPLE_CHUNKS_PER_STEP)

    return (y_prompt, y_sample.reshape(nb, nt, d), s_prompt[None], s_sample[None],
            v_rows.reshape(1, nb, nt, CMLP_WIDTH))
```
